```python
import jax
import jax.numpy as jnp
from jax import lax
import numpy as np

D_MODEL = 4096
BATCH = 2
SEQ = 4096
DEPTH = 1
DEC_BATCH = 32
DEC_SEQ = 8
PAST_LEN = 8192
PAGE_SIZE = 128

RW_WIDTH = D_MODEL // 2
RW_HEAD_DIM = 64
RW_HEADS = RW_WIDTH // RW_HEAD_DIM
DECAY_LORA = 128
ICLR_LORA = 128
RW_PROJ = 3 * RW_WIDTH + DECAY_LORA + ICLR_LORA
RW_GN_EPS = 64e-5
SB_WIDTH = D_MODEL // 2
SB_HEAD_DIM = 128
SB_HEADS = SB_WIDTH // SB_HEAD_DIM
SB_PROJ = 3 * SB_WIDTH
Q_BLOCK = 128
SB_BIAS_INIT = -6.0
GATE_PROJ = 2 * D_MODEL
IN_PROJ = RW_PROJ + SB_PROJ + GATE_PROJ
N_GROUPS = 8
EXPERTS_PER_GROUP = 8
N_EXPERTS = N_GROUPS * EXPERTS_PER_GROUP
TOP_K = 2
D_EXPERT = D_MODEL // 4
MOE_BLOCK = 128
RMS_EPS = 1e-6
POOL_NUM = 5
POOL_DEN = 4

kernel_name = 'rwkv7_stickbreak_hmoe_step'


def rmsnorm(x, g):
    xf = x.astype(jnp.float32)
    y = xf * lax.rsqrt(jnp.mean(xf * xf, axis=-1, keepdims=True) + RMS_EPS)
    return (y * g.astype(jnp.float32)).astype(x.dtype)


def rwkv7_recurrence(r, log_w, k, v, a_vec, b_vec, s0):
    def step(S, inp):
        r_t, lw_t, k_t, v_t, a_t, b_t = inp
        sa = jnp.einsum('bhij,bhj->bhi', S, a_t)
        S = (S * jnp.exp(lw_t)[:, :, None, :] + sa[..., None] * b_t[:, :, None, :]
             + v_t[..., None] * k_t[:, :, None, :])
        return S, jnp.einsum('bhij,bhj->bhi', S, r_t)
    xs = tuple(jnp.moveaxis(t, 1, 0) for t in (r, log_w, k, v, a_vec, b_vec))
    S, ys = lax.scan(step, s0, xs)
    return jnp.moveaxis(ys, 0, 1), S


def rwkv7_branch(p, shift0, wkv0, mu, w0, w2, a0, a2, k_k, k_a, r_k, ln_g, ln_b):
    bsz, T, _ = p.shape
    prev = jnp.concatenate([shift0[:, None, :].astype(p.dtype), p[:, :-1]], axis=1)
    m = p + (prev - p) * mu
    r = m[..., :RW_WIDTH]
    k = m[..., RW_WIDTH:2 * RW_WIDTH]
    v = m[..., 2 * RW_WIDTH:3 * RW_WIDTH]
    wd = m[..., 3 * RW_WIDTH:3 * RW_WIDTH + DECAY_LORA]
    ad = m[..., 3 * RW_WIDTH + DECAY_LORA:]
    w = (w0 + jnp.tanh(wd) @ w2).astype(jnp.float32)
    log_w = -jnp.exp(-jax.nn.softplus(-w) - 0.5)
    a = jax.nn.sigmoid((a0 + ad @ a2).astype(jnp.float32))
    hs = lambda t: t.astype(jnp.float32).reshape(bsz, T, RW_HEADS, RW_HEAD_DIM)
    r, k, v, a, log_w = hs(r), hs(k), hs(v), hs(a), hs(log_w)
    kk = k * k_k.astype(jnp.float32).reshape(RW_HEADS, RW_HEAD_DIM)
    kk = kk / jnp.maximum(jnp.sqrt(jnp.sum(kk * kk, axis=-1, keepdims=True)), 1e-12)
    k = k * (1.0 + (a - 1.0) * k_a.astype(jnp.float32).reshape(RW_HEADS, RW_HEAD_DIM))
    y, wkv = rwkv7_recurrence(r, log_w, k, v, -kk, kk * a, wkv0.astype(jnp.float32))
    mean = jnp.mean(y, axis=-1, keepdims=True)
    var = jnp.mean(jnp.square(y - mean), axis=-1, keepdims=True)
    y = ((y - mean) * lax.rsqrt(var + RW_GN_EPS) * ln_g.astype(jnp.float32).reshape(RW_HEADS, RW_HEAD_DIM)
         + ln_b.astype(jnp.float32).reshape(RW_HEADS, RW_HEAD_DIM))
    y = y + jnp.sum(r * k * r_k.astype(jnp.float32), axis=-1, keepdims=True) * v
    return y.reshape(bsz, T, RW_WIDTH).astype(p.dtype), wkv, p[:, -1]


def stick_breaking_block(q, k, v, q_pos, k_pos, bias):
    z = (jnp.einsum('bqhd,bkhd->bhqk', q, k).astype(jnp.float32) * (SB_HEAD_DIM ** -0.5)
         + bias.astype(jnp.float32)[None, :, None, None])
    causal = (k_pos[None, :] < q_pos[:, None])[None, None]
    log_1m = jnp.where(causal, jax.nn.log_sigmoid(-z), 0.0)
    suffix = lax.cumsum(log_1m, axis=3, reverse=True) - log_1m
    att = jnp.where(causal, jnp.exp(jax.nn.log_sigmoid(z) + suffix), 0.0)
    return jnp.einsum('bhqk,bkhd->bqhd', att.astype(v.dtype), v)


def stick_breaking(q, k, v, q_pos, k_pos, bias):
    bsz, Tq, H, Dh = q.shape
    if Tq <= Q_BLOCK:
        return stick_breaking_block(q, k, v, q_pos, k_pos, bias)
    nb = Tq // Q_BLOCK
    qb = q.reshape(bsz, nb, Q_BLOCK, H, Dh).transpose(1, 0, 2, 3, 4)
    pb = q_pos.reshape(nb, Q_BLOCK)
    out = lax.map(lambda a: stick_breaking_block(a[0], k, v, a[1], k_pos, bias), (qb, pb))
    return out.transpose(1, 0, 2, 3, 4).reshape(bsz, Tq, H, Dh)


def hier_route(h, rg_w, rg_b, re_w, re_b):
    n = h.shape[0]
    gl = (h @ rg_w).astype(jnp.float32) + rg_b.astype(jnp.float32)
    el = ((h @ re_w).astype(jnp.float32) + re_b.astype(jnp.float32)).reshape(n, N_GROUPS, EXPERTS_PER_GROUP)
    g_val, g_idx = lax.top_k(gl, 1)
    g_prob = jnp.take_along_axis(jax.nn.softmax(gl, axis=-1), g_idx, axis=-1)
    el_sel = jnp.take_along_axis(el, g_idx[:, :, None], axis=1)[:, 0]
    top_v, top_i = lax.top_k(el_sel, TOP_K)
    weight = jax.nn.softmax(top_v, axis=-1) * g_prob
    expert = (g_idx * EXPERTS_PER_GROUP + top_i).astype(jnp.int32)
    return expert, weight


def moe_ffn(h, rg_w, rg_b, re_w, re_b, w_gate, w_up, w_down):
    n, d = h.shape
    expert, weight = hier_route(h, rg_w, rg_b, re_w, re_b)
    n_rows = n * TOP_K
    flat_e = expert.reshape(-1)
    flat_tok = jnp.repeat(jnp.arange(n, dtype=jnp.int32), TOP_K)
    flat_w = weight.reshape(-1)
    order = jnp.argsort(flat_e)
    e_sorted, tok_sorted, w_sorted = flat_e[order], flat_tok[order], flat_w[order]
    counts = jnp.bincount(flat_e, length=N_EXPERTS).astype(jnp.int32)
    padded = (counts + MOE_BLOCK - 1) // MOE_BLOCK * MOE_BLOCK
    pad_end = jnp.cumsum(padded)
    pad_start = pad_end - padded
    start = jnp.cumsum(counts) - counts
    dest = pad_start[e_sorted] + jnp.arange(n_rows, dtype=jnp.int32) - start[e_sorted]
    n_blocks = -(-n_rows // MOE_BLOCK) + N_EXPERTS
    row_tok = jnp.full((n_blocks * MOE_BLOCK,), n, jnp.int32).at[dest].set(tok_sorted)
    h_pad = jnp.concatenate([h, jnp.zeros((1, d), h.dtype)], axis=0)
    xb = h_pad[row_tok].reshape(n_blocks, MOE_BLOCK, d)
    block_e = jnp.minimum(jnp.searchsorted(pad_end, jnp.arange(n_blocks, dtype=jnp.int32) * MOE_BLOCK,
                                           side='right'), N_EXPERTS - 1)

    def expert_block(args):
        xblk, e = args
        return (jax.nn.silu(xblk @ w_gate[e]) * (xblk @ w_up[e])) @ w_down[e]

    yb = lax.map(expert_block, (xb, block_e))
    y_rows = yb.reshape(-1, d)[dest]
    return jax.ops.segment_sum(y_rows * w_sorted[:, None].astype(y_rows.dtype), tok_sorted, num_segments=n)


def trunk_layer(x, pos0, shift0, wkv0, past_k, past_v,
                norm1_g, w_in, rwkv_mu, rwkv_w0, rwkv_w2, rwkv_a0, rwkv_a2, rwkv_kk, rwkv_ka, rwkv_rk,
                rwkv_ln_g, rwkv_ln_b, sb_bias, w_up_a, w_up_b, w_out, norm2_g,
                rg_w, rg_b, re_w, re_b, e_gate, e_up, e_down):
    bsz, T, d = x.shape
    h = rmsnorm(x, norm1_g)
    proj = h @ w_in
    p_a = proj[..., :RW_PROJ]
    p_b = proj[..., RW_PROJ:RW_PROJ + SB_PROJ]
    g_a = proj[..., RW_PROJ + SB_PROJ:RW_PROJ + SB_PROJ + D_MODEL]
    g_b = proj[..., RW_PROJ + SB_PROJ + D_MODEL:]
    o_a, wkv_new, shift_new = rwkv7_branch(p_a, shift0, wkv0, rwkv_mu, rwkv_w0, rwkv_w2, rwkv_a0, rwkv_a2,
                                           rwkv_kk, rwkv_ka, rwkv_rk, rwkv_ln_g, rwkv_ln_b)
    q = p_b[..., :SB_WIDTH].reshape(bsz, T, SB_HEADS, SB_HEAD_DIM)
    k = p_b[..., SB_WIDTH:2 * SB_WIDTH].reshape(bsz, T, SB_HEADS, SB_HEAD_DIM)
    v = p_b[..., 2 * SB_WIDTH:].reshape(bsz, T, SB_HEADS, SB_HEAD_DIM)
    q_pos = pos0 + jnp.arange(T, dtype=jnp.int32)
    if past_k is None:
        keys, vals, k_pos = k, v, q_pos
    else:
        keys = jnp.concatenate([past_k.astype(k.dtype), k], axis=1)
        vals = jnp.concatenate([past_v.astype(v.dtype), v], axis=1)
        k_pos = jnp.arange(pos0 + T, dtype=jnp.int32)
    o_b = stick_breaking(q, keys, vals, q_pos, k_pos, sb_bias).reshape(bsz, T, SB_WIDTH)
    mixed = jax.nn.sigmoid(g_a) * (o_a @ w_up_a) + jax.nn.sigmoid(g_b) * (o_b @ w_up_b)
    x = x + mixed @ w_out
    h2 = rmsnorm(x, norm2_g).reshape(bsz * T, d)
    x = x + moe_ffn(h2, rg_w, rg_b, re_w, re_b, e_gate, e_up, e_down).reshape(bsz, T, d).astype(x.dtype)
    return x, k, v, wkv_new, shift_new


def setup_inputs(seed: int = 0) -> dict:
    key = jax.random.key(seed)
    ks = jax.random.split(key, 32)
    f32 = jnp.float32
    n_pages = PAST_LEN // PAGE_SIZE
    n_phys = (DEC_BATCH * n_pages * POOL_NUM) // POOL_DEN
    nrm = lambda kk, shape, s: jax.random.normal(kk, shape, f32) * s
    page_table = jax.random.permutation(ks[4], n_phys)[:DEC_BATCH * n_pages].reshape(DEC_BATCH, n_pages).astype(jnp.int32)
    return {
        'x_prompt': nrm(ks[0], (BATCH, SEQ, D_MODEL), 1.0),
        'x_sample': nrm(ks[1], (DEC_BATCH, DEC_SEQ, D_MODEL), 1.0),
        'cache_k': nrm(ks[2], (DEPTH, n_phys, PAGE_SIZE, SB_HEADS, SB_HEAD_DIM), 1.0),
        'cache_v': nrm(ks[3], (DEPTH, n_phys, PAGE_SIZE, SB_HEADS, SB_HEAD_DIM), 1.0),
        'state_wkv': nrm(ks[5], (DEPTH, DEC_BATCH, RW_HEADS, RW_HEAD_DIM, RW_HEAD_DIM), 0.1),
        'state_shift': nrm(ks[6], (DEPTH, DEC_BATCH, RW_PROJ), 1.0),
        'page_table': page_table,
        'norm1_g': 1.0 + nrm(ks[7], (DEPTH, D_MODEL), 0.02),
        'w_in': nrm(ks[8], (DEPTH, D_MODEL, IN_PROJ), D_MODEL ** -0.5),
        'rwkv_mu': jax.random.uniform(ks[9], (DEPTH, RW_PROJ), f32),
        'rwkv_w0': jax.random.uniform(ks[10], (DEPTH, RW_WIDTH), f32, -6.0, 1.0),
        'rwkv_w2': nrm(ks[11], (DEPTH, DECAY_LORA, RW_WIDTH), DECAY_LORA ** -0.5),
        'rwkv_a0': nrm(ks[12], (DEPTH, RW_WIDTH), 0.5),
        'rwkv_a2': nrm(ks[13], (DEPTH, ICLR_LORA, RW_WIDTH), 0.5 * ICLR_LORA ** -0.5),
        'rwkv_kk': 0.85 + nrm(ks[14], (DEPTH, RW_WIDTH), 0.05),
        'rwkv_ka': 1.0 + nrm(ks[15], (DEPTH, RW_WIDTH), 0.05),
        'rwkv_rk': nrm(ks[16], (DEPTH, RW_HEADS, RW_HEAD_DIM), 0.1),
        'rwkv_ln_g': 1.0 + nrm(ks[17], (DEPTH, RW_WIDTH), 0.02),
        'rwkv_ln_b': nrm(ks[18], (DEPTH, RW_WIDTH), 0.02),
        'sb_bias': SB_BIAS_INIT + nrm(ks[31], (DEPTH, SB_HEADS), 0.1),
        'w_up_a': nrm(ks[19], (DEPTH, RW_WIDTH, D_MODEL), RW_WIDTH ** -0.5),
        'w_up_b': nrm(ks[20], (DEPTH, SB_WIDTH, D_MODEL), SB_WIDTH ** -0.5),
        'w_out': nrm(ks[21], (DEPTH, D_MODEL, D_MODEL), D_MODEL ** -0.5),
        'norm2_g': 1.0 + nrm(ks[22], (DEPTH, D_MODEL), 0.02),
        'router_group_w': nrm(ks[23], (DEPTH, D_MODEL, N_GROUPS), D_MODEL ** -0.5),
        'router_group_b': nrm(ks[24], (DEPTH, N_GROUPS), 0.01),
        'router_expert_w': nrm(ks[25], (DEPTH, D_MODEL, N_EXPERTS), D_MODEL ** -0.5),
        'router_expert_b': nrm(ks[26], (DEPTH, N_EXPERTS), 0.01),
        'expert_w_gate': nrm(ks[27], (DEPTH, N_EXPERTS, D_MODEL, D_EXPERT), D_MODEL ** -0.5),
        'expert_w_up': nrm(ks[28], (DEPTH, N_EXPERTS, D_MODEL, D_EXPERT), D_MODEL ** -0.5),
        'expert_w_down': nrm(ks[29], (DEPTH, N_EXPERTS, D_EXPERT, D_MODEL), D_EXPERT ** -0.5),
        'normf_g': 1.0 + nrm(ks[30], (D_MODEL,), 0.02),
    }


def reference(x_prompt, x_sample, cache_k, cache_v, state_wkv, state_shift, page_table,
              norm1_g, w_in, rwkv_mu, rwkv_w0, rwkv_w2, rwkv_a0, rwkv_a2, rwkv_kk, rwkv_ka, rwkv_rk,
              rwkv_ln_g, rwkv_ln_b, sb_bias, w_up_a, w_up_b, w_out, norm2_g,
              router_group_w, router_group_b, router_expert_w, router_expert_b,
              expert_w_gate, expert_w_up, expert_w_down, normf_g):
    n_dec, n_pages = page_table.shape
    past_len = n_pages * PAGE_SIZE
    xp, xs = x_prompt, x_sample
    kp, vp, wp, sp = [], [], [], []
    ksm, vsm, wsm, ssm = [], [], [], []
    for l in range(DEPTH):
        lw = (norm1_g[l], w_in[l], rwkv_mu[l], rwkv_w0[l], rwkv_w2[l], rwkv_a0[l], rwkv_a2[l],
              rwkv_kk[l], rwkv_ka[l], rwkv_rk[l], rwkv_ln_g[l], rwkv_ln_b[l], sb_bias[l],
              w_up_a[l], w_up_b[l], w_out[l], norm2_g[l], router_group_w[l], router_group_b[l],
              router_expert_w[l], router_expert_b[l], expert_w_gate[l], expert_w_up[l], expert_w_down[l])
        shift0 = jnp.zeros((xp.shape[0], RW_PROJ), xp.dtype)
        wkv0 = jnp.zeros((xp.shape[0], RW_HEADS, RW_HEAD_DIM, RW_HEAD_DIM), jnp.float32)
        xp, k_n, v_n, wkv_n, sh_n = trunk_layer(xp, 0, shift0, wkv0, None, None, *lw)
        kp.append(k_n); vp.append(v_n); wp.append(wkv_n); sp.append(sh_n)
        past_k = cache_k[l][page_table].reshape(n_dec, past_len, SB_HEADS, SB_HEAD_DIM)
        past_v = cache_v[l][page_table].reshape(n_dec, past_len, SB_HEADS, SB_HEAD_DIM)
        xs, k_n, v_n, wkv_n, sh_n = trunk_layer(xs, past_len, state_shift[l], state_wkv[l], past_k, past_v, *lw)
        ksm.append(k_n); vsm.append(v_n); wsm.append(wkv_n); ssm.append(sh_n)
    y_prompt = rmsnorm(xp, normf_g)
    y_sample = rmsnorm(xs, normf_g)
    return (y_prompt, y_sample, jnp.stack(kp), jnp.stack(vp), jnp.stack(wp), jnp.stack(sp),
            jnp.stack(ksm), jnp.stack(vsm), jnp.stack(wsm), jnp.stack(ssm))
```

```python
import functools

import jax
import jax.numpy as jnp
from jax import lax
from jax.experimental import pallas as pl
from jax.experimental.pallas import tpu as pltpu

F32 = jnp.float32
BF16 = jnp.bfloat16

LANES = 128
RW_HEAD_DIM = 64
RW_CHUNK = 64
RW_INV_BLOCK = 16
RW_GN_EPS = 64e-5
RMS_EPS = 1e-6
TOP_K = 2
MOE_SUB = 128
MOE_UNIT = 3 * MOE_SUB
VMEM_LIMIT_MB = 56


def _params(sem, vmem_mb=VMEM_LIMIT_MB):
    return pltpu.CompilerParams(dimension_semantics=sem, vmem_limit_bytes=vmem_mb << 20)


def _tile(n, pref, mult=8):
    t = min(pref, n)
    while t >= mult:
        if n % t == 0 and t % mult == 0:
            return t
        t -= 1
    return n


def _dg(a, b, nt=False):
    dims = (((1,), (1,)), ((), ())) if nt else (((1,), (0,)), ((), ()))
    return lax.dot_general(a, b, dims, preferred_element_type=F32)


def _split(x):
    hi = x.astype(BF16)
    lo = (x - hi.astype(F32)).astype(BF16)
    return hi, lo


def _dot1(a, b, nt=False):
    return _dg(a.astype(BF16), b.astype(BF16), nt)


def _dot3(a, b, nt=False):
    ah, al = _split(a)
    bh, bl = _split(b)
    return _dg(ah, bh, nt) + (_dg(ah, bl, nt) + _dg(al, bh, nt))


def _dot2x(a, b_exact):
    ah, al = _split(a)
    return _dg(ah, b_exact) + _dg(al, b_exact)


def _sigmoid(x):
    return 1.0 / (1.0 + jnp.exp(-x))


def _softplus(x):
    return jnp.maximum(x, 0.0) + jnp.log1p(jnp.exp(-jnp.abs(x)))


def _rmsnorm_kernel(x_ref, g_ref, o_ref):
    x = x_ref[...]
    ms = jnp.mean(x * x, axis=-1, keepdims=True)
    o_ref[...] = (x * lax.rsqrt(ms + RMS_EPS) * g_ref[...]).astype(o_ref.dtype)


def _rmsnorm(x, g, out_dtype):
    m, d = x.shape
    tm = _tile(m, 256)
    return pl.pallas_call(
        _rmsnorm_kernel,
        grid=(m // tm,),
        in_specs=[pl.BlockSpec((tm, d), lambda i: (i, 0)),
                  pl.BlockSpec((1, d), lambda i: (0, 0))],
        out_specs=pl.BlockSpec((tm, d), lambda i: (i, 0)),
        out_shape=jax.ShapeDtypeStruct((m, d), out_dtype),
        compiler_params=_params(("parallel",)),
        name="rmsnorm",
    )(x, g.reshape(1, d))


def _mm_kernel(a_ref, b_ref, o_ref):
    o_ref[...] = jnp.dot(a_ref[...], b_ref[...], preferred_element_type=F32).astype(o_ref.dtype)


def _matmul(a, b, out_dtype, tm_pref, tn_pref):
    m, k = a.shape
    n = b.shape[1]
    tm = _tile(m, tm_pref, 16)
    tn = _tile(n, tn_pref, LANES)
    return pl.pallas_call(
        _mm_kernel,
        grid=(m // tm, n // tn),
        in_specs=[pl.BlockSpec((tm, k), lambda i, j: (i, 0)),
                  pl.BlockSpec((k, tn), lambda i, j: (0, j))],
        out_specs=pl.BlockSpec((tm, tn), lambda i, j: (i, j)),
        out_shape=jax.ShapeDtypeStruct((m, n), out_dtype),
        compiler_params=_params(("parallel", "arbitrary")),
        name="in_proj",
    )(a, b)


def _merge_kernel(oa_ref, ob_ref, wa_ref, wb_ref, ga_ref, gb_ref, o_ref):
    ya = jnp.dot(oa_ref[...], wa_ref[...], preferred_element_type=F32)
    yb = jnp.dot(ob_ref[...], wb_ref[...], preferred_element_type=F32)
    o_ref[...] = (_sigmoid(ga_ref[...]) * ya + _sigmoid(gb_ref[...]) * yb).astype(o_ref.dtype)


def _merge_up(o_a, o_b, proj, w_up_a, w_up_b, ga_off, gb_off):
    m, ka = o_a.shape
    kb = o_b.shape[1]
    d = w_up_a.shape[1]
    tm = _tile(m, 1024, 16)
    tn = 256
    assert d % tn == 0 and ga_off % tn == 0 and gb_off % tn == 0
    ga_blk, gb_blk = ga_off // tn, gb_off // tn
    return pl.pallas_call(
        _merge_kernel,
        grid=(m // tm, d // tn),
        in_specs=[pl.BlockSpec((tm, ka), lambda i, j: (i, 0)),
                  pl.BlockSpec((tm, kb), lambda i, j: (i, 0)),
                  pl.BlockSpec((ka, tn), lambda i, j: (0, j)),
                  pl.BlockSpec((kb, tn), lambda i, j: (0, j)),
                  pl.BlockSpec((tm, tn), lambda i, j: (i, ga_blk + j)),
                  pl.BlockSpec((tm, tn), lambda i, j: (i, gb_blk + j))],
        out_specs=pl.BlockSpec((tm, tn), lambda i, j: (i, j)),
        out_shape=jax.ShapeDtypeStruct((m, d), BF16),
        compiler_params=_params(("parallel", "arbitrary")),
        name="merge_up",
    )(o_a, o_b, w_up_a, w_up_b, proj, proj)


def _outproj_kernel(a_ref, b_ref, x_ref, o_ref):
    o_ref[...] = x_ref[...] + jnp.dot(a_ref[...], b_ref[...], preferred_element_type=F32)


def _out_proj(mixed, w_out, x):
    m, k = mixed.shape
    n = w_out.shape[1]
    tm = _tile(m, 1024, 16)
    tn = _tile(n, 512, LANES)
    return pl.pallas_call(
        _outproj_kernel,
        grid=(m // tm, n // tn),
        in_specs=[pl.BlockSpec((tm, k), lambda i, j: (i, 0)),
                  pl.BlockSpec((k, tn), lambda i, j: (0, j)),
                  pl.BlockSpec((tm, tn), lambda i, j: (i, j))],
        out_specs=pl.BlockSpec((tm, tn), lambda i, j: (i, j)),
        out_shape=jax.ShapeDtypeStruct((m, n), F32),
        compiler_params=_params(("parallel", "arbitrary")),
        name="out_proj",
    )(mixed, w_out, x)


def _rwkv_prep_kernel(p_ref, sh_ref, mu_ref, w0_ref, w2_ref, a0_ref, a2_ref, kkw_ref, kaw_ref,
                      rkw_ref, ones_ref,
                      r_out, lw_out, k_out, v_out, na_out, bb_out, bon_out,
                      carry, *, rww, dl, al):
    t = pl.program_id(1)
    tt = p_ref.shape[1]

    @pl.when(t == 0)
    def _():
        carry[0:1, :] = sh_ref[0]

    row0 = lax.broadcasted_iota(jnp.int32, (tt, 1), 0) == 0

    def mixed(lo, hi):
        p = p_ref[0, :, lo:hi]
        prev = jnp.where(row0, carry[0:1, lo:hi], pltpu.roll(p, 1, 0))
        return p + (prev - p) * mu_ref[:, lo:hi]

    wd = mixed(3 * rww, 3 * rww + dl)
    ad = mixed(3 * rww + dl, 3 * rww + dl + al)
    w_lora = _dot3(jnp.tanh(wd), w2_ref[...])
    a_lora = _dot3(ad, a2_ref[...])
    ones = ones_ref[...]
    for s in range(rww // LANES):
        c0, c1 = s * LANES, (s + 1) * LANES
        r = mixed(c0, c1)
        k = mixed(rww + c0, rww + c1)
        v = mixed(2 * rww + c0, 2 * rww + c1)
        w = w0_ref[:, c0:c1] + w_lora[:, c0:c1]
        lw = -jnp.exp(-_softplus(-w) - 0.5)
        a = _sigmoid(a0_ref[:, c0:c1] + a_lora[:, c0:c1])
        kk = k * kkw_ref[:, c0:c1]
        ss = _dot2x(kk * kk, ones)
        kk = kk / jnp.maximum(jnp.sqrt(ss), 1e-12)
        k2 = k * (1.0 + (a - 1.0) * kaw_ref[:, c0:c1])
        rk = _dot2x(r * k2 * rkw_ref[:, c0:c1], ones)
        r_out[0, :, c0:c1] = r
        lw_out[0, :, c0:c1] = lw
        k_out[0, :, c0:c1] = k2
        v_out[0, :, c0:c1] = v
        na_out[0, :, c0:c1] = -kk
        bb_out[0, :, c0:c1] = kk * a
        bon_out[0, :, c0:c1] = rk * v
    carry[0:1, :] = p_ref[0, tt - 1:tt, :]


def _head_ones():
    i = jnp.arange(LANES) // RW_HEAD_DIM
    return (i[:, None] == i[None, :]).astype(BF16)


def _rwkv_prep(proj, shift0, mu, w0, w2, a0, a2, kkw, kaw, rkw):
    b, t, _ = proj.shape
    p = mu.shape[0]
    rww = w0.shape[0]
    dl, al = w2.shape[0], a2.shape[0]
    tt = _tile(t, 128)
    row = lambda x: x.reshape(1, -1)
    out = jax.ShapeDtypeStruct((b, t, rww), F32)
    vec = lambda n: pl.BlockSpec((1, n), lambda i, j: (0, 0))
    full = lambda s: pl.BlockSpec(s, lambda i, j: (0, 0))
    blk = pl.BlockSpec((1, tt, rww), lambda i, j: (i, j, 0))
    return pl.pallas_call(
        functools.partial(_rwkv_prep_kernel, rww=rww, dl=dl, al=al),
        grid=(b, t // tt),
        in_specs=[pl.BlockSpec((1, tt, p), lambda i, j: (i, j, 0)),
                  pl.BlockSpec((1, 1, p), lambda i, j: (i, 0, 0)),
                  vec(p), vec(rww), full((dl, rww)), vec(rww), full((al, rww)),
                  vec(rww), vec(rww), vec(rww), full((LANES, LANES))],
        out_specs=[blk] * 7,
        out_shape=[out] * 7,
        scratch_shapes=[pltpu.VMEM((8, p), F32)],
        compiler_params=_params(("parallel", "arbitrary")),
        name="rwkv_prep",
    )(proj, shift0.reshape(b, 1, p), row(mu), row(w0), w2, row(a0), a2, row(kkw), row(kaw),
      row(rkw), _head_ones())


def _tri_inv(a, n, nb):
    row = lax.broadcasted_iota(jnp.int32, (n, n), 0)
    col = lax.broadcasted_iota(jnp.int32, (n, n), 1)
    dmask = (row // RW_INV_BLOCK) == (col // RW_INV_BLOCK)
    eye = jnp.where(row == col, 1.0, 0.0)
    ad = jnp.where(dmask, a, 0.0)
    ao = jnp.where(dmask, 0.0, a)
    dinv = eye + ad
    p = ad
    k = 1
    while 2 * k < RW_INV_BLOCK:
        p = _dot3(p, p)
        dinv = dinv + _dot3(dinv, p)
        k *= 2
    bm = _dot3(dinv, ao)
    x = eye + bm
    p = bm
    k = 1
    while 2 * k < nb:
        p = _dot3(p, p)
        x = x + _dot3(x, p)
        k *= 2
    return _dot3(x, dinv)


def _rwkv_chunk_kernel(r_ref, lw_ref, k_ref, v_ref, na_ref, bb_ref, bon_ref, lng_ref, lnb_ref,
                       s0_ref, ltri_ref, ones_ref, o_ref, sf_ref, s_scr):
    c = pl.program_id(2)
    ch = r_ref.shape[1]
    n = 2 * ch

    @pl.when(c == 0)
    def _():
        s_scr[...] = s0_ref[0, 0]

    lw = lw_ref[0]
    cum = _dot2x_lhs(ltri_ref[...], lw)
    cl = cum[ch - 1:ch, :]
    e_neg = jnp.exp(-cum)
    e_end = jnp.exp(cl - cum)
    kk = k_ref[0]
    bb = bb_ref[0]
    a_t = na_ref[0] * jnp.exp(cum - lw)
    r_t = r_ref[0] * jnp.exp(cum)
    lane_lo = lax.broadcasted_iota(jnp.int32, (ch, LANES), 1) < RW_HEAD_DIM

    def hat(x):
        return jnp.concatenate([jnp.where(lane_lo, x, 0.0), jnp.where(lane_lo, 0.0, x)], axis=0)

    ath, rth, vh = hat(a_t), hat(r_t), hat(v_ref[0])
    bth, kth = hat(bb * e_neg), hat(kk * e_neg)
    bbh, kbh = hat(bb * e_end), hat(kk * e_end)

    row = lax.broadcasted_iota(jnp.int32, (n, n), 0)
    col = lax.broadcasted_iota(jnp.int32, (n, n), 1)
    strict = row > col
    incl = row >= col
    sc = _dot3(jnp.concatenate([ath, rth], axis=0), jnp.concatenate([bth, kth], axis=0), nt=True)
    a_ab = jnp.where(strict, sc[:n, :n], 0.0)
    a_ak = jnp.where(strict, sc[:n, n:], 0.0)
    a_rb = jnp.where(incl, sc[n:, :n], 0.0)
    a_rk = jnp.where(incl, sc[n:, n:], 0.0)

    tinv = _tri_inv(a_ab, n, ch // RW_INV_BLOCK)
    x = _dot3(tinv, jnp.concatenate([_dot3(a_ak, vh), ath], axis=1))
    u0, a2 = x[:, :LANES], x[:, LANES:]
    z = _dot3(a_rb, x)
    y0 = z[:, :LANES] + _dot3(a_rk, vh)
    rh = rth + z[:, LANES:]

    s = s_scr[...]
    yh = _dot3(rh, s, nt=True) + y0
    y = yh[:ch] + yh[ch:]

    m = jnp.where(row == col, jnp.exp(cl), 0.0) + _dot3(a2.T, bbh)
    nn = _dot3(jnp.concatenate([u0.T, vh.T], axis=1), jnp.concatenate([bbh, kbh], axis=0))
    s_new = _dot3(s, m) + nn
    s_scr[...] = s_new
    sf_ref[0, 0] = s_new

    ones = ones_ref[...]
    mean = _dot2x(y, ones) * (1.0 / RW_HEAD_DIM)
    d = y - mean
    var = _dot2x(d * d, ones) * (1.0 / RW_HEAD_DIM)
    out = d * lax.rsqrt(var + RW_GN_EPS) * lng_ref[...] + lnb_ref[...] + bon_ref[0]
    o_ref[0] = out.astype(o_ref.dtype)


def _dot2x_lhs(a_exact, b):
    bh, bl = _split(b)
    return _dg(a_exact, bh) + _dg(a_exact, bl)


def _rwkv_recurrence(r, lw, k, v, na, bb, bonus, wkv0, ln_g, ln_b):
    b, t, rww = r.shape
    hp = rww // LANES
    hd = RW_HEAD_DIM
    ch = RW_CHUNK
    tp = -(-t // ch) * ch
    if tp != t:
        pad = lambda x: jnp.pad(x, ((0, 0), (0, tp - t), (0, 0)))
        r, lw, k, v, na, bb, bonus = map(pad, (r, lw, k, v, na, bb, bonus))
    nc = tp // ch
    s0 = jnp.zeros((b, hp, 2, hd, 2, hd), F32)
    w0 = wkv0.astype(F32).reshape(b, hp, 2, hd, hd)
    s0 = s0.at[:, :, 0, :, 0, :].set(w0[:, :, 0]).at[:, :, 1, :, 1, :].set(w0[:, :, 1])
    s0 = s0.reshape(b, hp, LANES, LANES)
    ltri = (jnp.arange(ch)[:, None] >= jnp.arange(ch)[None, :]).astype(BF16)
    blk = pl.BlockSpec((1, ch, LANES), lambda i, j, c: (i, c, j))
    vec = pl.BlockSpec((1, LANES), lambda i, j, c: (0, j))
    st = pl.BlockSpec((1, 1, LANES, LANES), lambda i, j, c: (i, j, 0, 0))
    full = lambda s: pl.BlockSpec(s, lambda i, j, c: (0, 0))
    o, sf = pl.pallas_call(
        _rwkv_chunk_kernel,
        grid=(b, hp, nc),
        in_specs=[blk] * 7 + [vec, vec, st, full((ch, ch)), full((LANES, LANES))],
        out_specs=[blk, st],
        out_shape=[jax.ShapeDtypeStruct((b, tp, rww), BF16),
                   jax.ShapeDtypeStruct((b, hp, LANES, LANES), F32)],
        scratch_shapes=[pltpu.VMEM((LANES, LANES), F32)],
        compiler_params=_params(("parallel", "parallel", "arbitrary")),
        name="rwkv_chunk",
    )(r, lw, k, v, na, bb, bonus, ln_g.reshape(1, rww), ln_b.reshape(1, rww), s0, ltri, _head_ones())
    sf = sf.reshape(b, hp, 2, hd, 2, hd)
    wkv = jnp.stack([sf[:, :, 0, :, 0, :], sf[:, :, 1, :, 1, :]], axis=2).reshape(b, 2 * hp, hd, hd)
    return o[:, :t], wkv


def _suffix_matrix(tk):
    i = jnp.arange(tk)
    upper = (i[:, None] > i[None, :]).astype(BF16)
    return jnp.concatenate([upper, jnp.ones((tk, tk), BF16)], axis=1)


def _sb_block(q, k, v, bias, mxo, carry, acc, mask):
    tk = k.shape[0]
    z = _dg(q, k, nt=True) + bias
    sp = _softplus(z)
    lsm = -sp if mask is None else jnp.where(mask, -sp, 0.0)
    st = _dg(lsm.astype(BF16), mxo)
    att = jnp.exp((z - sp) + st[:, :tk] + carry[...])
    if mask is not None:
        att = jnp.where(mask, att, 0.0)
    acc[...] += _dg(att.astype(BF16), v)
    carry[...] += st[:, tk:]


def _sb_prompt_kernel(bias_ref, q_ref, k_ref, v_ref, mxo_ref, o_ref, acc, carry, *, scale):
    h = pl.program_id(1)
    qi = pl.program_id(2)
    tq = q_ref.shape[1]
    bias = bias_ref[h]
    q = (q_ref[0] * scale).astype(BF16)
    mxo = mxo_ref[...]
    acc[...] = jnp.zeros_like(acc)
    carry[...] = jnp.zeros_like(carry)

    def kv(kb):
        start = pl.multiple_of(kb * tq, tq)
        return (k_ref[0, pl.ds(start, tq), :].astype(BF16), v_ref[0, pl.ds(start, tq), :].astype(BF16))

    row = lax.broadcasted_iota(jnp.int32, (tq, tq), 0)
    col = lax.broadcasted_iota(jnp.int32, (tq, tq), 1)
    kd, vd = kv(qi)
    _sb_block(q, kd, vd, bias, mxo, carry, acc, col < row)

    def body(i, c):
        kb, vb = kv(qi - 1 - i)
        _sb_block(q, kb, vb, bias, mxo, carry, acc, None)
        return c

    lax.fori_loop(0, qi, body, 0)
    o_ref[0] = acc[...].astype(o_ref.dtype)


def _sb_prompt(proj, bias, q_off, k_off, v_off, heads, hd):
    b, t, _ = proj.shape
    tq = _tile(t, 256, LANES)
    qb, kb, vb = q_off // hd, k_off // hd, v_off // hd
    return pl.pallas_call(
        functools.partial(_sb_prompt_kernel, scale=hd ** -0.5),
        grid=(b, heads, t // tq),
        in_specs=[pl.BlockSpec(memory_space=pltpu.SMEM),
                  pl.BlockSpec((1, tq, hd), lambda i, h, j: (i, j, qb + h)),
                  pl.BlockSpec((1, t, hd), lambda i, h, j: (i, 0, kb + h)),
                  pl.BlockSpec((1, t, hd), lambda i, h, j: (i, 0, vb + h)),
                  pl.BlockSpec((tq, 2 * tq), lambda i, h, j: (0, 0))],
        out_specs=pl.BlockSpec((1, tq, hd), lambda i, h, j: (i, j, h)),
        out_shape=jax.ShapeDtypeStruct((b, t, heads * hd), BF16),
        scratch_shapes=[pltpu.VMEM((tq, hd), F32), pltpu.VMEM((tq, tq), F32)],
        compiler_params=_params(("parallel", "parallel", "arbitrary")),
        name="sb_prompt",
    )(bias.astype(F32), proj, proj, proj, _suffix_matrix(tq))


SB_QPAD = 16


def _sb_sample_kernel(pt_ref, q_ref, kn_ref, vn_ref, kp_ref, vp_ref, bias_ref, mxo_ref, o_ref,
                      zbuf, abuf, acc, carry, *, heads, hd, tq, scale):
    j = pl.program_id(1)
    nj = pl.num_programs(1)
    qp = SB_QPAD
    rows = heads * qp

    @pl.when(j == 0)
    def _():
        acc[...] = jnp.zeros_like(acc)
        carry[...] = jnp.zeros_like(carry)

    def process(k_ref, v_ref, masked):
        tk = k_ref.shape[1]
        for h in range(heads):
            c0, c1 = h * hd, (h + 1) * hd
            qh = jnp.concatenate([q_ref[0, :, c0:c1] * scale, jnp.zeros((qp - tq, hd), F32)], axis=0)
            zbuf[h * qp:(h + 1) * qp, :] = _dg(qh.astype(BF16), k_ref[0, :, c0:c1].astype(BF16), nt=True)
        z = zbuf[...] + bias_ref[...]
        sp = _softplus(z)
        if masked:
            qidx = lax.broadcasted_iota(jnp.int32, (rows, tk), 0) % qp
            kidx = lax.broadcasted_iota(jnp.int32, (rows, tk), 1)
            mask = kidx < qidx
            lsm = jnp.where(mask, -sp, 0.0)
        else:
            lsm = -sp
        st = _dg(lsm.astype(BF16), mxo_ref[...])
        att = jnp.exp((z - sp) + st[:, :tk] + carry[...])
        if masked:
            att = jnp.where(mask, att, 0.0)
        abuf[...] = att
        carry[...] += st[:, tk:]
        for h in range(heads):
            c0, c1 = h * hd, (h + 1) * hd
            ah = abuf[h * qp:(h + 1) * qp, :].astype(BF16)
            acc[h * qp:(h + 1) * qp, :] += _dg(ah, v_ref[0, :, c0:c1].astype(BF16))

    @pl.when(j == 0)
    def _():
        process(kn_ref, vn_ref, True)

    @pl.when(j > 0)
    def _():
        process(kp_ref, vp_ref, False)

    @pl.when(j == nj - 1)
    def _():
        for h in range(heads):
            o_ref[0, :, h * hd:(h + 1) * hd] = acc[h * qp:h * qp + tq, :]


def _sb_sample(q, k_new, v_new, cache_k, cache_v, page_table, bias):
    nb, tq, sbw = q.shape
    nphys, page, heads, hd = cache_k.shape
    n_pages = page_table.shape[1]
    assert tq <= SB_QPAD and tq <= page and page % LANES == 0
    pad = lambda x: jnp.pad(x, ((0, 0), (0, page - tq), (0, 0)))
    ck = cache_k.reshape(nphys, page, sbw)
    cv = cache_v.reshape(nphys, page, sbw)
    rows = heads * SB_QPAD
    bias_rows = jnp.broadcast_to(jnp.repeat(bias.astype(F32), SB_QPAD)[:, None], (rows, page))

    def page_map(i, j, pt):
        return (pt[i * n_pages + n_pages - jnp.maximum(j, 1)], 0, 0)

    seq = lambda i, j, pt: (i, 0, 0)
    const = lambda i, j, pt: (0, 0)
    return pl.pallas_call(
        functools.partial(_sb_sample_kernel, heads=heads, hd=hd, tq=tq, scale=hd ** -0.5),
        grid_spec=pltpu.PrefetchScalarGridSpec(
            num_scalar_prefetch=1,
            grid=(nb, n_pages + 1),
            in_specs=[pl.BlockSpec((1, tq, sbw), seq),
                      pl.BlockSpec((1, page, sbw), seq),
                      pl.BlockSpec((1, page, sbw), seq),
                      pl.BlockSpec((1, page, sbw), page_map),
                      pl.BlockSpec((1, page, sbw), page_map),
                      pl.BlockSpec((rows, page), const),
                      pl.BlockSpec((page, 2 * page), const)],
            out_specs=pl.BlockSpec((1, tq, sbw), seq),
            scratch_shapes=[pltpu.VMEM((rows, page), F32), pltpu.VMEM((rows, page), F32),
                            pltpu.VMEM((rows, hd), F32), pltpu.VMEM((rows, page), F32)]),
        out_shape=jax.ShapeDtypeStruct((nb, tq, sbw), F32),
        compiler_params=_params(("parallel", "arbitrary")),
        name="sb_sample",
    )(page_table.reshape(-1).astype(jnp.int32), q, pad(k_new), pad(v_new), ck, cv, bias_rows,
      _suffix_matrix(page))


def _router_kernel(x_ref, g_ref, rw_ref, rb_ref, h_ref, id_ref, wt_ref, *, n_groups, per_group):
    x = x_ref[...]
    ms = jnp.mean(x * x, axis=-1, keepdims=True)
    h = x * lax.rsqrt(ms + RMS_EPS) * g_ref[...]
    h_ref[...] = h
    logits = _dot3(h, rw_ref[...]) + rb_ref[...]
    tm = logits.shape[0]
    lane = lax.broadcasted_iota(jnp.int32, (tm, LANES), 1)
    lane_f = lane.astype(F32)
    neg = jnp.finfo(F32).min
    far = float(LANES)
    is_g = lane < n_groups
    gl = jnp.where(is_g, logits, neg)
    gmax = jnp.max(gl, axis=1, keepdims=True)
    gidx = jnp.min(jnp.where(gl == gmax, lane_f, far), axis=1, keepdims=True).astype(jnp.int32)
    gprob = 1.0 / jnp.sum(jnp.where(is_g, jnp.exp(logits - gmax), 0.0), axis=1, keepdims=True)
    lo = n_groups + gidx * per_group
    el = jnp.where((lane >= lo) & (lane < lo + per_group), logits, neg)
    v1 = jnp.max(el, axis=1, keepdims=True)
    i1 = jnp.min(jnp.where(el == v1, lane_f, far), axis=1, keepdims=True).astype(jnp.int32)
    el2 = jnp.where(lane == i1, neg, el)
    v2 = jnp.max(el2, axis=1, keepdims=True)
    i2 = jnp.min(jnp.where((el2 == v2) & (lane != i1), lane_f, far), axis=1, keepdims=True).astype(jnp.int32)
    t = jnp.exp(v2 - v1)
    w1 = gprob / (1.0 + t)
    w2 = gprob * t / (1.0 + t)
    id_ref[...] = jnp.where(lane == 0, i1 - n_groups, jnp.where(lane == 1, i2 - n_groups, 0))
    wt_ref[...] = jnp.where(lane == 0, w1, jnp.where(lane == 1, w2, 0.0))


def _router(x, g, rg_w, rg_b, re_w, re_b):
    m, d = x.shape
    n_groups = rg_w.shape[1]
    n_exp = re_w.shape[1]
    assert n_groups + n_exp <= LANES
    padc = LANES - n_groups - n_exp
    rw = jnp.concatenate([rg_w, re_w, jnp.zeros((d, padc), F32)], axis=1).astype(F32)
    rb = jnp.concatenate([rg_b, re_b, jnp.zeros((padc,), F32)]).reshape(1, LANES).astype(F32)
    tm = _tile(m, 256)
    row = pl.BlockSpec((tm, d), lambda i: (i, 0))
    nar = pl.BlockSpec((tm, LANES), lambda i: (i, 0))
    h, ids, wts = pl.pallas_call(
        functools.partial(_router_kernel, n_groups=n_groups, per_group=n_exp // n_groups),
        grid=(m // tm,),
        in_specs=[row, pl.BlockSpec((1, d), lambda i: (0, 0)),
                  pl.BlockSpec((d, LANES), lambda i: (0, 0)),
                  pl.BlockSpec((1, LANES), lambda i: (0, 0))],
        out_specs=[row, nar, nar],
        out_shape=[jax.ShapeDtypeStruct((m, d), F32),
                   jax.ShapeDtypeStruct((m, LANES), jnp.int32),
                   jax.ShapeDtypeStruct((m, LANES), F32)],
        compiler_params=_params(("parallel",)),
        name="router",
    )(x, g.reshape(1, d), rw, rb)
    return h, ids[:, :TOP_K], wts


def _moe_kernel(ue_ref, us_ref, un_ref, tok_ref, h_hbm, wg_ref, wu_ref, wd_ref, y_ref,
                xf, xb, wgb, wub, wdb, sem):
    u = pl.program_id(0)
    f = pl.program_id(1)
    n = un_ref[u]

    def row_copy(tok, i):
        return pltpu.make_async_copy(h_hbm.at[pl.ds(tok, 1), :], xf.at[pl.ds(i, 1), :], sem)

    @pl.when((f == 0) & (n > 0))
    def _():
        xf[...] = jnp.zeros_like(xf)
        base = us_ref[u]

        def start(i, c):
            row_copy(tok_ref[base + i], i).start()
            return c

        def wait(i, c):
            row_copy(0, i).wait()
            return c

        lax.fori_loop(0, n, start, 0)
        lax.fori_loop(0, n, wait, 0)
        xb[...] = xf[...].astype(BF16)

    @pl.when((f == 0) & (n == 0))
    def _():
        y_ref[...] = jnp.zeros_like(y_ref)

    @pl.when(n > 0)
    def _():
        wgb[...] = wg_ref[0].astype(BF16)
        wub[...] = wu_ref[0].astype(BF16)
        wdb[...] = wd_ref[0].astype(BF16)
        for sb in range(MOE_UNIT // MOE_SUB):
            rows = pl.ds(sb * MOE_SUB, MOE_SUB)

            @pl.when(sb * MOE_SUB < n)
            def _():
                x = xb[rows, :]
                g = _dg(x, wgb[...])
                up = _dg(x, wub[...])
                hid = (g * _sigmoid(g) * up).astype(BF16)
                part = _dg(hid, wdb[...])

                @pl.when(f == 0)
                def _():
                    y_ref[rows, :] = part

                @pl.when(f > 0)
                def _():
                    y_ref[rows, :] += part

            @pl.when((sb * MOE_SUB >= n) & (f == 0))
            def _():
                y_ref[rows, :] = jnp.zeros((MOE_SUB, y_ref.shape[1]), F32)


def _moe_experts(h, tok_sorted, unit_e, unit_start, unit_n, w_gate, w_up, w_down):
    m, d = h.shape
    n_units = unit_e.shape[0]
    de = w_gate.shape[2]
    tf = _tile(de, 256, LANES)
    nf = de // tf

    def fidx(u, f, un):
        return jnp.where(un[u] > 0, f, nf - 1)

    return pl.pallas_call(
        _moe_kernel,
        grid_spec=pltpu.PrefetchScalarGridSpec(
            num_scalar_prefetch=4,
            grid=(n_units, nf),
            in_specs=[pl.BlockSpec(memory_space=pl.ANY),
                      pl.BlockSpec((1, d, tf), lambda u, f, ue, us, un, tok: (ue[u], 0, fidx(u, f, un))),
                      pl.BlockSpec((1, d, tf), lambda u, f, ue, us, un, tok: (ue[u], 0, fidx(u, f, un))),
                      pl.BlockSpec((1, tf, d), lambda u, f, ue, us, un, tok: (ue[u], fidx(u, f, un), 0))],
            out_specs=pl.BlockSpec((MOE_UNIT, d), lambda u, f, ue, us, un, tok: (u, 0)),
            scratch_shapes=[pltpu.VMEM((MOE_UNIT, d), F32), pltpu.VMEM((MOE_UNIT, d), BF16),
                            pltpu.VMEM((d, tf), BF16), pltpu.VMEM((d, tf), BF16),
                            pltpu.VMEM((tf, d), BF16), pltpu.SemaphoreType.DMA(())]),
        out_shape=jax.ShapeDtypeStruct((n_units * MOE_UNIT, d), F32),
        compiler_params=_params(("arbitrary", "arbitrary")),
        name="moe_experts",
    )(unit_e, unit_start, unit_n, tok_sorted, h, w_gate, w_up, w_down)


def _combine_kernel(pos_ref, ys_hbm, w_ref, x_ref, g_ref, o_ref, buf, sem, *, normalize):
    i = pl.program_id(0)
    tc = x_ref.shape[0]
    base = i * tc * TOP_K

    def row_copy(p, s, t):
        return pltpu.make_async_copy(ys_hbm.at[pl.ds(p, 1), :], buf.at[s, pl.ds(t, 1), :], sem)

    def start(t, c):
        for s in range(TOP_K):
            row_copy(pos_ref[base + TOP_K * t + s], s, t).start()
        return c

    def wait(t, c):
        for s in range(TOP_K):
            row_copy(0, s, t).wait()
        return c

    lax.fori_loop(0, tc, start, 0)
    lax.fori_loop(0, tc, wait, 0)
    w = w_ref[...]
    x = x_ref[...] + (w[:, 0:1] * buf[0] + w[:, 1:2] * buf[1])
    if normalize:
        ms = jnp.mean(x * x, axis=-1, keepdims=True)
        x = x * lax.rsqrt(ms + RMS_EPS) * g_ref[...]
    o_ref[...] = x


def _moe_combine(ys, pos, wts, x, g, normalize):
    m, d = x.shape
    tc = _tile(m, 128)
    return pl.pallas_call(
        functools.partial(_combine_kernel, normalize=normalize),
        grid_spec=pltpu.PrefetchScalarGridSpec(
            num_scalar_prefetch=1,
            grid=(m // tc,),
            in_specs=[pl.BlockSpec(memory_space=pl.ANY),
                      pl.BlockSpec((tc, LANES), lambda i, p: (i, 0)),
                      pl.BlockSpec((tc, d), lambda i, p: (i, 0)),
                      pl.BlockSpec((1, d), lambda i, p: (0, 0))],
            out_specs=pl.BlockSpec((tc, d), lambda i, p: (i, 0)),
            scratch_shapes=[pltpu.VMEM((TOP_K, tc, d), F32), pltpu.SemaphoreType.DMA(())]),
        out_shape=jax.ShapeDtypeStruct((m, d), F32),
        compiler_params=_params(("arbitrary",)),
        name="moe_combine",
    )(pos, ys, wts, x, g.reshape(1, d))


def _moe_layer(x, norm_g, rg_w, rg_b, re_w, re_b, w_gate, w_up, w_down, final_g, normalize):
    m, d = x.shape
    n_exp = re_w.shape[1]
    h, expert, wts = _router(x, norm_g, rg_w, rg_b, re_w, re_b)

    n_rows = m * TOP_K
    flat_e = expert.reshape(-1)
    order = jnp.argsort(flat_e).astype(jnp.int32)
    e_sorted = flat_e[order]
    tok_sorted = order // TOP_K
    counts = jnp.zeros((n_exp,), jnp.int32).at[flat_e].add(1)
    start = jnp.cumsum(counts) - counts
    units_e = (counts + MOE_UNIT - 1) // MOE_UNIT
    ucum = jnp.cumsum(units_e)
    ufirst = ucum - units_e
    n_units = n_exp + -(-n_rows // MOE_UNIT)
    uid = jnp.arange(n_units, dtype=jnp.int32)
    total = ucum[-1]
    ue = jnp.minimum(jnp.searchsorted(ucum, uid, side="right"), n_exp - 1).astype(jnp.int32)
    local = uid - ufirst[ue]
    active = uid < total
    un = jnp.where(active, jnp.clip(counts[ue] - local * MOE_UNIT, 0, MOE_UNIT), 0).astype(jnp.int32)
    us = jnp.where(active, start[ue] + local * MOE_UNIT, 0).astype(jnp.int32)
    last_e = ue[jnp.maximum(total - 1, 0)]
    ue = jnp.where(active, ue, last_e).astype(jnp.int32)
    local_row = jnp.arange(n_rows, dtype=jnp.int32) - start[e_sorted]
    pos_sorted = (ufirst[e_sorted] + local_row // MOE_UNIT) * MOE_UNIT + local_row % MOE_UNIT
    pos = jnp.zeros((n_rows,), jnp.int32).at[order].set(pos_sorted.astype(jnp.int32))

    ys = _moe_experts(h, tok_sorted, ue, us, un, w_gate, w_up, w_down)
    return _moe_combine(ys, pos, wts, x, final_g, normalize)


def _mixer(x, shift0, wkv0, past, lw):
    b, t, d = x.shape
    rw_proj = lw["mu"].shape[0]
    rww = lw["w_up_a"].shape[0]
    sbw = lw["w_up_b"].shape[0]
    heads = lw["sb_bias"].shape[0]
    hd = sbw // heads
    q_off, k_off, v_off = rw_proj, rw_proj + sbw, rw_proj + 2 * sbw
    ga_off, gb_off = rw_proj + 3 * sbw, rw_proj + 3 * sbw + d
    x2 = x.reshape(b * t, d)

    h = _rmsnorm(x2, lw["norm1_g"], BF16)
    proj = _matmul(h, lw["w_in"], F32, 1024, 768)
    proj3 = proj.reshape(b, t, -1)

    r, lgw, k2, v, na, bb, bonus = _rwkv_prep(proj3, shift0, lw["mu"], lw["w0"], lw["w2"], lw["a0"],
                                              lw["a2"], lw["kk"], lw["ka"], lw["rk"])
    o_a, wkv_new = _rwkv_recurrence(r, lgw, k2, v, na, bb, bonus, wkv0, lw["ln_g"], lw["ln_b"])
    shift_new = proj3[:, -1, :rw_proj]

    k_new = proj3[:, :, k_off:k_off + sbw]
    v_new = proj3[:, :, v_off:v_off + sbw]
    if past is None:
        o_b = _sb_prompt(proj3, lw["sb_bias"], q_off, k_off, v_off, heads, hd)
    else:
        cache_k, cache_v, page_table = past
        q = proj3[:, :, q_off:q_off + sbw]
        o_b = _sb_sample(q, k_new, v_new, cache_k, cache_v, page_table, lw["sb_bias"]).astype(BF16)

    mixed = _merge_up(o_a.reshape(b * t, rww), o_b.reshape(b * t, sbw), proj, lw["w_up_a"], lw["w_up_b"],
                      ga_off, gb_off)
    x1 = _out_proj(mixed, lw["w_out"], x2)
    return (x1, k_new.reshape(b, t, heads, hd), v_new.reshape(b, t, heads, hd), wkv_new, shift_new)


def kernel(x_prompt, x_sample, cache_k, cache_v, state_wkv, state_shift, page_table, norm1_g, w_in, rwkv_mu, rwkv_w0, rwkv_w2, rwkv_a0, rwkv_a2, rwkv_kk, rwkv_ka, rwkv_rk, rwkv_ln_g, rwkv_ln_b, sb_bias, w_up_a, w_up_b, w_out, norm2_g, router_group_w, router_group_b, router_expert_w, router_expert_b, expert_w_gate, expert_w_up, expert_w_down, normf_g):
    depth = w_in.shape[0]
    bp, tp, d = x_prompt.shape
    bs, ts, _ = x_sample.shape
    rw_proj = rwkv_mu.shape[1]
    rww = w_up_a.shape[1]
    rw_heads = rww // RW_HEAD_DIM
    xp, xs = x_prompt, x_sample
    outs = [[] for _ in range(8)]
    for l in range(depth):
        lw = dict(norm1_g=norm1_g[l], w_in=w_in[l].astype(BF16), mu=rwkv_mu[l], w0=rwkv_w0[l],
                  w2=rwkv_w2[l], a0=rwkv_a0[l], a2=rwkv_a2[l], kk=rwkv_kk[l], ka=rwkv_ka[l],
                  rk=rwkv_rk[l].reshape(-1), ln_g=rwkv_ln_g[l], ln_b=rwkv_ln_b[l], sb_bias=sb_bias[l],
                  w_up_a=w_up_a[l].astype(BF16), w_up_b=w_up_b[l].astype(BF16),
                  w_out=w_out[l].astype(BF16))
        shift0 = jnp.zeros((bp, rw_proj), F32)
        wkv0 = jnp.zeros((bp, rw_heads, RW_HEAD_DIM, RW_HEAD_DIM), F32)
        res_p = _mixer(xp, shift0, wkv0, None, lw)
        res_s = _mixer(xs, state_shift[l], state_wkv[l], (cache_k[l], cache_v[l], page_table), lw)
        for i in range(4):
            outs[i].append(res_p[1 + i])
            outs[4 + i].append(res_s[1 + i])
        x1 = jnp.concatenate([res_p[0], res_s[0]], axis=0)
        last = l == depth - 1
        x2 = _moe_layer(x1, norm2_g[l], router_group_w[l], router_group_b[l], router_expert_w[l],
                        router_expert_b[l], expert_w_gate[l], expert_w_up[l], expert_w_down[l],
                        normf_g, last)
        xp = x2[:bp * tp].reshape(bp, tp, d)
        xs = x2[bp * tp:].reshape(bs, ts, d)
    return (xp, xs) + tuple(jnp.stack(o) for o in outs)
```

```python
import functools

import jax
import jax.numpy as jnp
from jax import lax
from jax.experimental import pallas as pl
from jax.experimental.pallas import tpu as pltpu

F32 = jnp.float32
BF16 = jnp.bfloat16

LANES = 128
RW_HEAD_DIM = 64
RW_CHUNK = 64
RW_INV_BLOCK = 16
RW_PAIRS_PER_STEP = 8
RW_GN_EPS = 64e-5
RMS_EPS = 1e-6
TOP_K = 2
MOE_SUB = 128
MOE_UNIT = 3 * MOE_SUB
VMEM_LIMIT_MB = 56


def _params(sem, vmem_mb=VMEM_LIMIT_MB):
    return pltpu.CompilerParams(dimension_semantics=sem, vmem_limit_bytes=vmem_mb << 20)


def _tile(n, pref, mult=8):
    t = min(pref, n)
    while t >= mult:
        if n % t == 0 and t % mult == 0:
            return t
        t -= 1
    return n


def _dg(a, b, nt=False):
    if a.ndim == 3:
        dims = (((2,), (2 if nt else 1,)), ((0,), (0,)))
    else:
        dims = (((1,), (1 if nt else 0,)), ((), ()))
    return lax.dot_general(a, b, dims, preferred_element_type=F32)


def _split(x):
    hi = x.astype(BF16)
    lo = (x - hi.astype(F32)).astype(BF16)
    return hi, lo


def _dot1(a, b, nt=False):
    return _dg(a.astype(BF16), b.astype(BF16), nt)


def _dot3(a, b, nt=False):
    ah, al = _split(a)
    bh, bl = _split(b)
    return _dg(ah, bh, nt) + (_dg(ah, bl, nt) + _dg(al, bh, nt))


def _dot2x(a, b_exact):
    ah, al = _split(a)
    return _dg(ah, b_exact) + _dg(al, b_exact)


def _sigmoid(x):
    return 1.0 / (1.0 + jnp.exp(-x))


def _softplus(x):
    return jnp.maximum(x, 0.0) + jnp.log1p(jnp.exp(-jnp.abs(x)))


def _rmsnorm_kernel(x_ref, g_ref, o_ref):
    x = x_ref[...]
    ms = jnp.mean(x * x, axis=-1, keepdims=True)
    o_ref[...] = (x * lax.rsqrt(ms + RMS_EPS) * g_ref[...]).astype(o_ref.dtype)


def _rmsnorm(x, g, out_dtype):
    m, d = x.shape
    tm = _tile(m, 256)
    return pl.pallas_call(
        _rmsnorm_kernel,
        grid=(m // tm,),
        in_specs=[pl.BlockSpec((tm, d), lambda i: (i, 0)),
                  pl.BlockSpec((1, d), lambda i: (0, 0))],
        out_specs=pl.BlockSpec((tm, d), lambda i: (i, 0)),
        out_shape=jax.ShapeDtypeStruct((m, d), out_dtype),
        compiler_params=_params(("parallel",)),
        name="rmsnorm",
    )(x, g.reshape(1, d))


def _mm_kernel(a_ref, b_ref, o_ref):
    o_ref[...] = jnp.dot(a_ref[...], b_ref[...], preferred_element_type=F32).astype(o_ref.dtype)


def _matmul(a, b, out_dtype, tm_pref, tn_pref):
    m, k = a.shape
    n = b.shape[1]
    tm = _tile(m, tm_pref, 16)
    tn = _tile(n, tn_pref, LANES)
    return pl.pallas_call(
        _mm_kernel,
        grid=(m // tm, n // tn),
        in_specs=[pl.BlockSpec((tm, k), lambda i, j: (i, 0)),
                  pl.BlockSpec((k, tn), lambda i, j: (0, j))],
        out_specs=pl.BlockSpec((tm, tn), lambda i, j: (i, j)),
        out_shape=jax.ShapeDtypeStruct((m, n), out_dtype),
        compiler_params=_params(("parallel", "arbitrary")),
        name="in_proj",
    )(a, b)


def _merge_kernel(oa_ref, ob_ref, wa_ref, wb_ref, ga_ref, gb_ref, o_ref):
    ya = jnp.dot(oa_ref[...], wa_ref[...], preferred_element_type=F32)
    yb = jnp.dot(ob_ref[...], wb_ref[...], preferred_element_type=F32)
    o_ref[...] = (_sigmoid(ga_ref[...]) * ya + _sigmoid(gb_ref[...]) * yb).astype(o_ref.dtype)


def _merge_up(o_a, o_b, proj, w_up_a, w_up_b, ga_off, gb_off):
    m, ka = o_a.shape
    kb = o_b.shape[1]
    d = w_up_a.shape[1]
    tm = _tile(m, 1024, 16)
    tn = 256
    assert d % tn == 0 and ga_off % tn == 0 and gb_off % tn == 0
    ga_blk, gb_blk = ga_off // tn, gb_off // tn
    return pl.pallas_call(
        _merge_kernel,
        grid=(m // tm, d // tn),
        in_specs=[pl.BlockSpec((tm, ka), lambda i, j: (i, 0)),
                  pl.BlockSpec((tm, kb), lambda i, j: (i, 0)),
                  pl.BlockSpec((ka, tn), lambda i, j: (0, j)),
                  pl.BlockSpec((kb, tn), lambda i, j: (0, j)),
                  pl.BlockSpec((tm, tn), lambda i, j: (i, ga_blk + j)),
                  pl.BlockSpec((tm, tn), lambda i, j: (i, gb_blk + j))],
        out_specs=pl.BlockSpec((tm, tn), lambda i, j: (i, j)),
        out_shape=jax.ShapeDtypeStruct((m, d), BF16),
        compiler_params=_params(("parallel", "arbitrary")),
        name="merge_up",
    )(o_a, o_b, w_up_a, w_up_b, proj, proj)


def _outproj_kernel(a_ref, b_ref, x_ref, o_ref):
    o_ref[...] = x_ref[...] + jnp.dot(a_ref[...], b_ref[...], preferred_element_type=F32)


def _out_proj(mixed, w_out, x):
    m, k = mixed.shape
    n = w_out.shape[1]
    tm = _tile(m, 1024, 16)
    tn = _tile(n, 512, LANES)
    return pl.pallas_call(
        _outproj_kernel,
        grid=(m // tm, n // tn),
        in_specs=[pl.BlockSpec((tm, k), lambda i, j: (i, 0)),
                  pl.BlockSpec((k, tn), lambda i, j: (0, j)),
                  pl.BlockSpec((tm, tn), lambda i, j: (i, j))],
        out_specs=pl.BlockSpec((tm, tn), lambda i, j: (i, j)),
        out_shape=jax.ShapeDtypeStruct((m, n), F32),
        compiler_params=_params(("parallel", "arbitrary")),
        name="out_proj",
    )(mixed, w_out, x)


def _rwkv_prep_kernel(p_ref, sh_ref, mu_ref, w0_ref, w2_ref, a0_ref, a2_ref, kkw_ref, kaw_ref,
                      rkw_ref, ones_ref,
                      r_out, lw_out, k_out, v_out, na_out, bb_out, bon_out,
                      carry, *, rww, dl, al):
    t = pl.program_id(1)
    tt = p_ref.shape[1]

    @pl.when(t == 0)
    def _():
        carry[0:1, :] = sh_ref[0]

    row0 = lax.broadcasted_iota(jnp.int32, (tt, 1), 0) == 0

    def mixed(lo, hi):
        p = p_ref[0, :, lo:hi]
        prev = jnp.where(row0, carry[0:1, lo:hi], pltpu.roll(p, 1, 0))
        return p + (prev - p) * mu_ref[:, lo:hi]

    wd = mixed(3 * rww, 3 * rww + dl)
    ad = mixed(3 * rww + dl, 3 * rww + dl + al)
    w_lora = _dot3(jnp.tanh(wd), w2_ref[...])
    a_lora = _dot3(ad, a2_ref[...])
    ones = ones_ref[...]
    for s in range(rww // LANES):
        c0, c1 = s * LANES, (s + 1) * LANES
        r = mixed(c0, c1)
        k = mixed(rww + c0, rww + c1)
        v = mixed(2 * rww + c0, 2 * rww + c1)
        w = w0_ref[:, c0:c1] + w_lora[:, c0:c1]
        lw = -jnp.exp(-_softplus(-w) - 0.5)
        a = _sigmoid(a0_ref[:, c0:c1] + a_lora[:, c0:c1])
        kk = k * kkw_ref[:, c0:c1]
        ss = _dot2x(kk * kk, ones)
        kk = kk / jnp.maximum(jnp.sqrt(ss), 1e-12)
        k2 = k * (1.0 + (a - 1.0) * kaw_ref[:, c0:c1])
        rk = _dot2x(r * k2 * rkw_ref[:, c0:c1], ones)
        r_out[0, :, c0:c1] = r
        lw_out[0, :, c0:c1] = lw
        k_out[0, :, c0:c1] = k2
        v_out[0, :, c0:c1] = v
        na_out[0, :, c0:c1] = -kk
        bb_out[0, :, c0:c1] = kk * a
        bon_out[0, :, c0:c1] = rk * v
    carry[0:1, :] = p_ref[0, tt - 1:tt, :]


def _head_ones():
    i = jnp.arange(LANES) // RW_HEAD_DIM
    return (i[:, None] == i[None, :]).astype(BF16)


def _rwkv_prep(proj, shift0, mu, w0, w2, a0, a2, kkw, kaw, rkw):
    b, t, _ = proj.shape
    p = mu.shape[0]
    rww = w0.shape[0]
    dl, al = w2.shape[0], a2.shape[0]
    tt = _tile(t, 128)
    row = lambda x: x.reshape(1, -1)
    out = jax.ShapeDtypeStruct((b, t, rww), F32)
    vec = lambda n: pl.BlockSpec((1, n), lambda i, j: (0, 0))
    full = lambda s: pl.BlockSpec(s, lambda i, j: (0, 0))
    blk = pl.BlockSpec((1, tt, rww), lambda i, j: (i, j, 0))
    return pl.pallas_call(
        functools.partial(_rwkv_prep_kernel, rww=rww, dl=dl, al=al),
        grid=(b, t // tt),
        in_specs=[pl.BlockSpec((1, tt, p), lambda i, j: (i, j, 0)),
                  pl.BlockSpec((1, 1, p), lambda i, j: (i, 0, 0)),
                  vec(p), vec(rww), full((dl, rww)), vec(rww), full((al, rww)),
                  vec(rww), vec(rww), vec(rww), full((LANES, LANES))],
        out_specs=[blk] * 7,
        out_shape=[out] * 7,
        scratch_shapes=[pltpu.VMEM((8, p), F32)],
        compiler_params=_params(("parallel", "arbitrary")),
        name="rwkv_prep",
    )(proj, shift0.reshape(b, 1, p), row(mu), row(w0), w2, row(a0), a2, row(kkw), row(kaw),
      row(rkw), _head_ones())


def _tri_inv(a, n, nb):
    row = lax.broadcasted_iota(jnp.int32, (n, n), 0)
    col = lax.broadcasted_iota(jnp.int32, (n, n), 1)
    dmask = (row // RW_INV_BLOCK) == (col // RW_INV_BLOCK)
    eye = jnp.where(row == col, 1.0, 0.0)
    ad = jnp.where(dmask, a, 0.0)
    ao = jnp.where(dmask, 0.0, a)
    dinv = eye + ad
    p = ad
    k = 1
    while 2 * k < RW_INV_BLOCK:
        p = _dot3(p, p)
        dinv = dinv + _dot3(dinv, p)
        k *= 2
    bm = _dot3(dinv, ao)
    x = eye + bm
    p = bm
    k = 1
    while 2 * k < nb:
        p = _dot3(p, p)
        x = x + _dot3(x, p)
        k *= 2
    return _dot3(x, dinv)


def _rwkv_chunk_math(r, cum, lw, kk, v, na, bb, s):
    ch = r.shape[1]
    n = 2 * ch
    cl = cum[:, ch - 1:ch, :]
    e_neg = jnp.exp(-cum)
    e_end = jnp.exp(cl - cum)
    a_t = na * jnp.exp(cum - lw)
    r_t = r * jnp.exp(cum)
    lane_lo = lax.broadcasted_iota(jnp.int32, (ch, LANES), 1) < RW_HEAD_DIM

    def hat(x):
        return jnp.concatenate([jnp.where(lane_lo, x, 0.0), jnp.where(lane_lo, 0.0, x)], axis=1)

    ath, rth, vh = hat(a_t), hat(r_t), hat(v)
    btkt = jnp.concatenate([hat(bb * e_neg), hat(kk * e_neg)], axis=1)
    bbh, kbh = hat(bb * e_end), hat(kk * e_end)

    row = lax.broadcasted_iota(jnp.int32, (n, n), 0)
    col = lax.broadcasted_iota(jnp.int32, (n, n), 1)
    strict = row > col
    incl = row >= col
    sc_a = _dot3(ath, btkt, nt=True)
    sc_r = _dot1(rth, btkt, nt=True)
    a_ab = jnp.where(strict, sc_a[:, :, :n], 0.0)
    a_ak = jnp.where(strict, sc_a[:, :, n:], 0.0)
    a_rb = jnp.where(incl, sc_r[:, :, :n], 0.0)
    a_rk = jnp.where(incl, sc_r[:, :, n:], 0.0)

    tinv = _tri_inv(a_ab, n, ch // RW_INV_BLOCK)
    x = _dot3(tinv, jnp.concatenate([_dot1(a_ak, vh), ath], axis=2))
    u0, a2 = x[:, :, :LANES], x[:, :, LANES:]
    z = _dot1(a_rb, x)
    y0 = z[:, :, :LANES] + _dot1(a_rk, vh)
    rh = rth + z[:, :, LANES:]
    yh = _dot1(rh, s, nt=True) + y0
    y = yh[:, :ch] + yh[:, ch:]

    tr = lambda t: jnp.swapaxes(t, 1, 2)
    m = jnp.where(row == col, jnp.exp(cl), 0.0) + _dot3(tr(a2), bbh)
    nn = _dot1(jnp.concatenate([tr(u0), tr(vh)], axis=2), jnp.concatenate([bbh, kbh], axis=1))
    return y, _dot3(s, m) + nn


def _rwkv_chunk_kernel(r_ref, lw_ref, k_ref, v_ref, na_ref, bb_ref, bon_ref, lng_ref, lnb_ref,
                       s0_ref, ltri_ref, ones_ref, o_ref, sf_ref, s_scr):
    c = pl.program_id(2)
    pairs = s_scr.shape[0]
    ch = r_ref.shape[1]

    @pl.when(c == 0)
    def _():
        s_scr[...] = s0_ref[0]

    def pairwise(x):
        return jnp.stack([x[:, g * LANES:(g + 1) * LANES] for g in range(pairs)], axis=0)

    lw = lw_ref[0]
    cum = _dot2x_lhs(ltri_ref[...], lw)
    y, s_new = _rwkv_chunk_math(pairwise(r_ref[0]), pairwise(cum), pairwise(lw), pairwise(k_ref[0]),
                                pairwise(v_ref[0]), pairwise(na_ref[0]), pairwise(bb_ref[0]), s_scr[...])
    s_scr[...] = s_new
    sf_ref[0] = s_new

    ones = ones_ref[...]
    y2 = y.reshape(pairs * ch, LANES)
    mean = _dot2x(y2, ones) * (1.0 / RW_HEAD_DIM)
    d = y2 - mean
    var = _dot2x(d * d, ones) * (1.0 / RW_HEAD_DIM)
    yn = d * lax.rsqrt(var + RW_GN_EPS)
    for g in range(pairs):
        sl = slice(g * LANES, (g + 1) * LANES)
        out = yn[g * ch:(g + 1) * ch] * lng_ref[:, sl] + lnb_ref[:, sl] + bon_ref[0, :, sl]
        o_ref[0, :, sl] = out.astype(o_ref.dtype)


def _dot2x_lhs(a_exact, b):
    bh, bl = _split(b)
    return _dg(a_exact, bh) + _dg(a_exact, bl)


def _rwkv_recurrence(r, lw, k, v, na, bb, bonus, wkv0, ln_g, ln_b):
    b, t, rww = r.shape
    hp = rww // LANES
    hd = RW_HEAD_DIM
    ch = RW_CHUNK
    tp = -(-t // ch) * ch
    if tp != t:
        pad = lambda x: jnp.pad(x, ((0, 0), (0, tp - t), (0, 0)))
        r, lw, k, v, na, bb, bonus = map(pad, (r, lw, k, v, na, bb, bonus))
    nc = tp // ch
    s0 = jnp.zeros((b, hp, 2, hd, 2, hd), F32)
    w0 = wkv0.astype(F32).reshape(b, hp, 2, hd, hd)
    s0 = s0.at[:, :, 0, :, 0, :].set(w0[:, :, 0]).at[:, :, 1, :, 1, :].set(w0[:, :, 1])
    s0 = s0.reshape(b, hp, LANES, LANES)
    ltri = (jnp.arange(ch)[:, None] >= jnp.arange(ch)[None, :]).astype(BF16)
    pairs = _tile(hp, RW_PAIRS_PER_STEP, 1)
    blk = pl.BlockSpec((1, ch, pairs * LANES), lambda i, j, c: (i, c, j))
    vec = pl.BlockSpec((1, pairs * LANES), lambda i, j, c: (0, j))
    st = pl.BlockSpec((1, pairs, LANES, LANES), lambda i, j, c: (i, j, 0, 0))
    full = lambda s: pl.BlockSpec(s, lambda i, j, c: (0, 0))
    o, sf = pl.pallas_call(
        _rwkv_chunk_kernel,
        grid=(b, hp // pairs, nc),
        in_specs=[blk] * 7 + [vec, vec, st, full((ch, ch)), full((LANES, LANES))],
        out_specs=[blk, st],
        out_shape=[jax.ShapeDtypeStruct((b, tp, rww), BF16),
                   jax.ShapeDtypeStruct((b, hp, LANES, LANES), F32)],
        scratch_shapes=[pltpu.VMEM((pairs, LANES, LANES), F32)],
        compiler_params=_params(("parallel", "parallel", "arbitrary")),
        name="rwkv_chunk",
    )(r, lw, k, v, na, bb, bonus, ln_g.reshape(1, rww), ln_b.reshape(1, rww), s0, ltri, _head_ones())
    sf = sf.reshape(b, hp, 2, hd, 2, hd)
    wkv = jnp.stack([sf[:, :, 0, :, 0, :], sf[:, :, 1, :, 1, :]], axis=2).reshape(b, 2 * hp, hd, hd)
    return o[:, :t], wkv


def _suffix_matrix(tk):
    i = jnp.arange(tk)
    upper = (i[:, None] > i[None, :]).astype(BF16)
    return jnp.concatenate([upper, jnp.ones((tk, tk), BF16)], axis=1)


def _sb_block(q, k, v, bias, mxo, mask):
    tk = k.shape[0]
    hd = v.shape[1]
    z = _dg(q, k, nt=True) + bias
    sp = _softplus(z)
    lsm = -sp if mask is None else jnp.where(mask, -sp, 0.0)
    st = _dg(lsm.astype(BF16), mxo)
    att = jnp.exp((z - sp) + st[:, :tk])
    if mask is not None:
        att = jnp.where(mask, att, 0.0)
    return _dg(att.astype(BF16), v), st[:, tk:tk + hd]


def _sb_prompt_kernel(bias_ref, q_ref, k_ref, v_ref, mxo_ref, o_ref, acc, carry, *, scale):
    h = pl.program_id(1)
    qi = pl.program_id(2)
    tq = q_ref.shape[1]
    bias = bias_ref[h]
    q = (q_ref[0] * scale).astype(BF16)
    mxo = mxo_ref[...]

    hd = q_ref.shape[2]

    def block(kb, mask=None):
        start = pl.multiple_of(kb * tq, tq)
        k = k_ref[0, pl.ds(start, tq), :].astype(BF16)
        v = v_ref[0, pl.ds(start, tq), :].astype(BF16)
        return _sb_block(q, k, v, bias, mxo, mask)

    row = lax.broadcasted_iota(jnp.int32, (tq, tq), 0)
    col = lax.broadcasted_iota(jnp.int32, (tq, tq), 1)
    pv, tot = block(qi, col < row)
    acc[...] = pv
    carry[...] = tot

    def pair(i, c):
        kb = qi - 1 - 2 * i
        start = pl.multiple_of((kb - 1) * tq, tq)
        k2 = k_ref[0, pl.ds(start, 2 * tq), :].astype(BF16)
        v2 = v_ref[0, pl.ds(start, 2 * tq), :].astype(BF16)
        z = _dg(q, k2, nt=True) + bias
        sp = _softplus(z)
        lsm = (-sp).astype(BF16)
        st = _dg(jnp.concatenate([lsm[:, :tq], lsm[:, tq:]], axis=0), mxo)
        st_old, st_new = st[:tq], st[tq:]
        la = z - sp
        att_old = jnp.exp(la[:, :tq] + st_old[:, :tq] + st_new[:, tq:])
        att_new = jnp.exp(la[:, tq:] + st_new[:, :tq])
        pv2 = _dg(jnp.concatenate([att_old, att_new], axis=1).astype(BF16), v2)
        cr = carry[...]
        acc[...] += jnp.exp(cr) * pv2
        carry[...] = cr + (st_old[:, tq:tq + hd] + st_new[:, tq:tq + hd])
        return c

    lax.fori_loop(0, qi // 2, pair, 0)

    @pl.when(qi % 2 == 1)
    def _():
        pv0, _ = block(0)
        acc[...] += jnp.exp(carry[...]) * pv0

    o_ref[0] = acc[...].astype(o_ref.dtype)


def _sb_prompt(proj, bias, q_off, k_off, v_off, heads, hd):
    b, t, _ = proj.shape
    tq = _tile(t, 256, LANES)
    qb, kb, vb = q_off // hd, k_off // hd, v_off // hd
    return pl.pallas_call(
        functools.partial(_sb_prompt_kernel, scale=hd ** -0.5),
        grid=(b, heads, t // tq),
        in_specs=[pl.BlockSpec(memory_space=pltpu.SMEM),
                  pl.BlockSpec((1, tq, hd), lambda i, h, j: (i, j, qb + h)),
                  pl.BlockSpec((1, t, hd), lambda i, h, j: (i, 0, kb + h)),
                  pl.BlockSpec((1, t, hd), lambda i, h, j: (i, 0, vb + h)),
                  pl.BlockSpec((tq, 2 * tq), lambda i, h, j: (0, 0))],
        out_specs=pl.BlockSpec((1, tq, hd), lambda i, h, j: (i, j, h)),
        out_shape=jax.ShapeDtypeStruct((b, t, heads * hd), BF16),
        scratch_shapes=[pltpu.VMEM((tq, hd), F32), pltpu.VMEM((tq, hd), F32)],
        compiler_params=_params(("parallel", "parallel", "arbitrary")),
        name="sb_prompt",
    )(bias.astype(F32), proj, proj, proj, _suffix_matrix(tq))


SB_QPAD = 16
SB_PAGES_PER_STEP = 4


def _sb_sample_kernel(pt_ref, q_ref, kn_ref, vn_ref, *refs, heads, hd, tq, scale, pps):
    kp_refs, vp_refs = refs[:pps], refs[pps:2 * pps]
    bias_ref, mxo_ref, o_ref, zbuf, abuf, pvbuf, acc, carry = refs[2 * pps:]
    j = pl.program_id(1)
    nj = pl.num_programs(1)
    qp = SB_QPAD
    rows = heads * qp

    def pages(k_refs, v_refs, masked):
        n = len(k_refs)
        tk = k_refs[0].shape[1]
        qs = []
        for h in range(heads):
            qh = jnp.concatenate([q_ref[0, :, h * hd:(h + 1) * hd] * scale,
                                  jnp.zeros((qp - tq, hd), F32)], axis=0)
            qs.append(qh.astype(BF16))
        for s in range(n):
            for h in range(heads):
                kh = k_refs[s][0, :, h * hd:(h + 1) * hd].astype(BF16)
                zbuf[s, h * qp:(h + 1) * qp, :] = _dg(qs[h], kh, nt=True)
        z = zbuf[0:n] + bias_ref[...]
        sp = _softplus(z)
        if masked:
            qidx = lax.broadcasted_iota(jnp.int32, (rows, tk), 0) % qp
            kidx = lax.broadcasted_iota(jnp.int32, (rows, tk), 1)
            mask = kidx < qidx
            lsm = jnp.where(mask, -sp, 0.0)
        else:
            lsm = -sp
        st = _dg(lsm.astype(BF16).reshape(n * rows, tk), mxo_ref[...]).reshape(n, rows, 2 * tk)
        att = jnp.exp((z - sp) + st[:, :, :tk])
        if masked:
            att = jnp.where(mask, att, 0.0)
        abuf[0:n] = att
        for s in range(n):
            for h in range(heads):
                ah = abuf[s, h * qp:(h + 1) * qp, :].astype(BF16)
                pvbuf[s, h * qp:(h + 1) * qp, :] = _dg(ah, v_refs[s][0, :, h * hd:(h + 1) * hd].astype(BF16))
        return st[:, :, tk:tk + hd]

    @pl.when(j == 0)
    def _():
        carry[...] = pages([kn_ref], [vn_ref], True)[0]
        acc[...] = pvbuf[0]

    @pl.when(j > 0)
    def _():
        tots = pages(kp_refs, vp_refs, False)
        cr = carry[...]
        out = acc[...]
        for s in range(pps):
            out = out + jnp.exp(cr) * pvbuf[s]
            cr = cr + tots[s]
        acc[...] = out
        carry[...] = cr

    @pl.when(j == nj - 1)
    def _():
        for h in range(heads):
            o_ref[0, :, h * hd:(h + 1) * hd] = acc[h * qp:h * qp + tq, :]


def _sb_sample(q, k_new, v_new, cache_k, cache_v, layer, page_table, bias):
    nb, tq, sbw = q.shape
    depth, nphys, page, heads, hd = cache_k.shape
    n_pages = page_table.shape[1]
    assert tq <= SB_QPAD and tq <= page and page % LANES == 0
    pad = lambda x: jnp.pad(x, ((0, 0), (0, page - tq), (0, 0)))
    ck = cache_k.reshape(depth * nphys, page, sbw)
    cv = cache_v.reshape(depth * nphys, page, sbw)
    page_table = page_table + layer * nphys
    rows = heads * SB_QPAD
    bias_rows = jnp.broadcast_to(jnp.repeat(bias.astype(F32), SB_QPAD)[:, None], (rows, page))
    pps = _tile(n_pages, SB_PAGES_PER_STEP, 1)

    def page_map(s):
        def index(i, j, pt):
            return (pt[i * n_pages + n_pages - 1 - (jnp.maximum(j, 1) - 1) * pps - s], 0, 0)
        return index

    seq = lambda i, j, pt: (i, 0, 0)
    const = lambda i, j, pt: (0, 0)
    pages = [pl.BlockSpec((1, page, sbw), page_map(s)) for s in range(pps)]
    return pl.pallas_call(
        functools.partial(_sb_sample_kernel, heads=heads, hd=hd, tq=tq, scale=hd ** -0.5, pps=pps),
        grid_spec=pltpu.PrefetchScalarGridSpec(
            num_scalar_prefetch=1,
            grid=(nb, n_pages // pps + 1),
            in_specs=[pl.BlockSpec((1, tq, sbw), seq),
                      pl.BlockSpec((1, page, sbw), seq),
                      pl.BlockSpec((1, page, sbw), seq)] + pages + pages +
                     [pl.BlockSpec((rows, page), const),
                      pl.BlockSpec((page, 2 * page), const)],
            out_specs=pl.BlockSpec((1, tq, sbw), seq),
            scratch_shapes=[pltpu.VMEM((pps, rows, page), F32), pltpu.VMEM((pps, rows, page), F32),
                            pltpu.VMEM((pps, rows, hd), F32),
                            pltpu.VMEM((rows, hd), F32), pltpu.VMEM((rows, hd), F32)]),
        out_shape=jax.ShapeDtypeStruct((nb, tq, sbw), F32),
        compiler_params=_params(("parallel", "arbitrary")),
        name="sb_sample",
    )(page_table.reshape(-1).astype(jnp.int32), q, pad(k_new), pad(v_new), *([ck] * pps), *([cv] * pps),
      bias_rows, _suffix_matrix(page))


def _router_kernel(x_ref, g_ref, rw_ref, rb_ref, h_ref, id_ref, wt_ref, *, n_groups, per_group):
    x = x_ref[...]
    ms = jnp.mean(x * x, axis=-1, keepdims=True)
    h = x * lax.rsqrt(ms + RMS_EPS) * g_ref[...]
    h_ref[...] = h
    logits = _dot3(h, rw_ref[...]) + rb_ref[...]
    tm = logits.shape[0]
    lane = lax.broadcasted_iota(jnp.int32, (tm, LANES), 1)
    lane_f = lane.astype(F32)
    neg = jnp.finfo(F32).min
    far = float(LANES)
    is_g = lane < n_groups
    gl = jnp.where(is_g, logits, neg)
    gmax = jnp.max(gl, axis=1, keepdims=True)
    gidx = jnp.min(jnp.where(gl == gmax, lane_f, far), axis=1, keepdims=True).astype(jnp.int32)
    gprob = 1.0 / jnp.sum(jnp.where(is_g, jnp.exp(logits - gmax), 0.0), axis=1, keepdims=True)
    lo = n_groups + gidx * per_group
    el = jnp.where((lane >= lo) & (lane < lo + per_group), logits, neg)
    v1 = jnp.max(el, axis=1, keepdims=True)
    i1 = jnp.min(jnp.where(el == v1, lane_f, far), axis=1, keepdims=True).astype(jnp.int32)
    el2 = jnp.where(lane == i1, neg, el)
    v2 = jnp.max(el2, axis=1, keepdims=True)
    i2 = jnp.min(jnp.where((el2 == v2) & (lane != i1), lane_f, far), axis=1, keepdims=True).astype(jnp.int32)
    t = jnp.exp(v2 - v1)
    w1 = gprob / (1.0 + t)
    w2 = gprob * t / (1.0 + t)
    id_ref[...] = jnp.where(lane == 0, i1 - n_groups, jnp.where(lane == 1, i2 - n_groups, 0))
    wt_ref[...] = jnp.where(lane == 0, w1, jnp.where(lane == 1, w2, 0.0))


def _router(x, g, rg_w, rg_b, re_w, re_b):
    m, d = x.shape
    n_groups = rg_w.shape[1]
    n_exp = re_w.shape[1]
    assert n_groups + n_exp <= LANES
    padc = LANES - n_groups - n_exp
    rw = jnp.concatenate([rg_w, re_w, jnp.zeros((d, padc), F32)], axis=1).astype(F32)
    rb = jnp.concatenate([rg_b, re_b, jnp.zeros((padc,), F32)]).reshape(1, LANES).astype(F32)
    tm = _tile(m, 256)
    row = pl.BlockSpec((tm, d), lambda i: (i, 0))
    nar = pl.BlockSpec((tm, LANES), lambda i: (i, 0))
    h, ids, wts = pl.pallas_call(
        functools.partial(_router_kernel, n_groups=n_groups, per_group=n_exp // n_groups),
        grid=(m // tm,),
        in_specs=[row, pl.BlockSpec((1, d), lambda i: (0, 0)),
                  pl.BlockSpec((d, LANES), lambda i: (0, 0)),
                  pl.BlockSpec((1, LANES), lambda i: (0, 0))],
        out_specs=[row, nar, nar],
        out_shape=[jax.ShapeDtypeStruct((m, d), F32),
                   jax.ShapeDtypeStruct((m, LANES), jnp.int32),
                   jax.ShapeDtypeStruct((m, LANES), F32)],
        compiler_params=_params(("parallel",)),
        name="router",
    )(x, g.reshape(1, d), rw, rb)
    return h, ids[:, :TOP_K], wts


def _moe_kernel(ue_ref, us_ref, un_ref, tok_ref, h_hbm, wg_ref, wu_ref, wd_ref, y_ref,
                xf, xb, wgb, wub, wdb, sem):
    u = pl.program_id(0)
    f = pl.program_id(1)
    n = un_ref[u]

    def row_copy(tok, i):
        return pltpu.make_async_copy(h_hbm.at[pl.ds(tok, 1), :], xf.at[pl.ds(i, 1), :], sem)

    @pl.when((f == 0) & (n > 0))
    def _():
        xf[...] = jnp.zeros_like(xf)
        base = us_ref[u]

        def start(i, c):
            row_copy(tok_ref[base + i], i).start()
            return c

        def wait(i, c):
            row_copy(0, i).wait()
            return c

        lax.fori_loop(0, n, start, 0)
        lax.fori_loop(0, n, wait, 0)
        xb[...] = xf[...].astype(BF16)

    @pl.when((f == 0) & (n == 0))
    def _():
        y_ref[...] = jnp.zeros_like(y_ref)

    @pl.when(n > 0)
    def _():
        wgb[...] = wg_ref[0].astype(BF16)
        wub[...] = wu_ref[0].astype(BF16)
        wdb[...] = wd_ref[0].astype(BF16)
        for sb in range(MOE_UNIT // MOE_SUB):
            rows = pl.ds(sb * MOE_SUB, MOE_SUB)

            @pl.when(sb * MOE_SUB < n)
            def _():
                x = xb[rows, :]
                g = _dg(x, wgb[...])
                up = _dg(x, wub[...])
                hid = (g * _sigmoid(g) * up).astype(BF16)
                part = _dg(hid, wdb[...])

                @pl.when(f == 0)
                def _():
                    y_ref[rows, :] = part

                @pl.when(f > 0)
                def _():
                    y_ref[rows, :] += part

            @pl.when((sb * MOE_SUB >= n) & (f == 0))
            def _():
                y_ref[rows, :] = jnp.zeros((MOE_SUB, y_ref.shape[1]), F32)


def _moe_experts(h, tok_sorted, unit_e, unit_start, unit_n, w_gate, w_up, w_down):
    m, d = h.shape
    n_units = unit_e.shape[0]
    de = w_gate.shape[2]
    tf = _tile(de, 256, LANES)
    nf = de // tf

    def fidx(u, f, un):
        return jnp.where(un[u] > 0, f, nf - 1)

    return pl.pallas_call(
        _moe_kernel,
        grid_spec=pltpu.PrefetchScalarGridSpec(
            num_scalar_prefetch=4,
            grid=(n_units, nf),
            in_specs=[pl.BlockSpec(memory_space=pl.ANY),
                      pl.BlockSpec((1, d, tf), lambda u, f, ue, us, un, tok: (ue[u], 0, fidx(u, f, un))),
                      pl.BlockSpec((1, d, tf), lambda u, f, ue, us, un, tok: (ue[u], 0, fidx(u, f, un))),
                      pl.BlockSpec((1, tf, d), lambda u, f, ue, us, un, tok: (ue[u], fidx(u, f, un), 0))],
            out_specs=pl.BlockSpec((MOE_UNIT, d), lambda u, f, ue, us, un, tok: (u, 0)),
            scratch_shapes=[pltpu.VMEM((MOE_UNIT, d), F32), pltpu.VMEM((MOE_UNIT, d), BF16),
                            pltpu.VMEM((d, tf), BF16), pltpu.VMEM((d, tf), BF16),
                            pltpu.VMEM((tf, d), BF16), pltpu.SemaphoreType.DMA(())]),
        out_shape=jax.ShapeDtypeStruct((n_units * MOE_UNIT, d), F32),
        compiler_params=_params(("arbitrary", "arbitrary")),
        name="moe_experts",
    )(unit_e, unit_start, unit_n, tok_sorted, h, w_gate, w_up, w_down)


def _combine_kernel(pos_ref, ys_hbm, w_ref, x_ref, g_ref, o_ref, buf, sem, *, normalize):
    i = pl.program_id(0)
    tc = x_ref.shape[0]
    base = i * tc * TOP_K

    def row_copy(p, s, t):
        return pltpu.make_async_copy(ys_hbm.at[pl.ds(p, 1), :], buf.at[s, pl.ds(t, 1), :], sem)

    def start(t, c):
        for s in range(TOP_K):
            row_copy(pos_ref[base + TOP_K * t + s], s, t).start()
        return c

    def wait(t, c):
        for s in range(TOP_K):
            row_copy(0, s, t).wait()
        return c

    lax.fori_loop(0, tc, start, 0)
    lax.fori_loop(0, tc, wait, 0)
    w = w_ref[...]
    x = x_ref[...] + (w[:, 0:1] * buf[0] + w[:, 1:2] * buf[1])
    if normalize:
        ms = jnp.mean(x * x, axis=-1, keepdims=True)
        x = x * lax.rsqrt(ms + RMS_EPS) * g_ref[...]
    o_ref[...] = x


def _moe_combine(ys, pos, wts, x, g, normalize):
    m, d = x.shape
    tc = _tile(m, 128)
    return pl.pallas_call(
        functools.partial(_combine_kernel, normalize=normalize),
        grid_spec=pltpu.PrefetchScalarGridSpec(
            num_scalar_prefetch=1,
            grid=(m // tc,),
            in_specs=[pl.BlockSpec(memory_space=pl.ANY),
                      pl.BlockSpec((tc, LANES), lambda i, p: (i, 0)),
                      pl.BlockSpec((tc, d), lambda i, p: (i, 0)),
                      pl.BlockSpec((1, d), lambda i, p: (0, 0))],
            out_specs=pl.BlockSpec((tc, d), lambda i, p: (i, 0)),
            scratch_shapes=[pltpu.VMEM((TOP_K, tc, d), F32), pltpu.SemaphoreType.DMA(())]),
        out_shape=jax.ShapeDtypeStruct((m, d), F32),
        compiler_params=_params(("arbitrary",)),
        name="moe_combine",
    )(pos, ys, wts, x, g.reshape(1, d))


def _moe_layer(x, norm_g, rg_w, rg_b, re_w, re_b, w_gate, w_up, w_down, final_g, normalize):
    m, d = x.shape
    n_exp = re_w.shape[1]
    h, expert, wts = _router(x, norm_g, rg_w, rg_b, re_w, re_b)

    n_rows = m * TOP_K
    flat_e = expert.reshape(-1)
    order = jnp.argsort(flat_e).astype(jnp.int32)
    e_sorted = flat_e[order]
    tok_sorted = order // TOP_K
    counts = jnp.zeros((n_exp,), jnp.int32).at[flat_e].add(1)
    start = jnp.cumsum(counts) - counts
    units_e = (counts + MOE_UNIT - 1) // MOE_UNIT
    ucum = jnp.cumsum(units_e)
    ufirst = ucum - units_e
    n_units = n_exp + -(-n_rows // MOE_UNIT)
    uid = jnp.arange(n_units, dtype=jnp.int32)
    total = ucum[-1]
    ue = jnp.minimum(jnp.searchsorted(ucum, uid, side="right"), n_exp - 1).astype(jnp.int32)
    local = uid - ufirst[ue]
    active = uid < total
    un = jnp.where(active, jnp.clip(counts[ue] - local * MOE_UNIT, 0, MOE_UNIT), 0).astype(jnp.int32)
    us = jnp.where(active, start[ue] + local * MOE_UNIT, 0).astype(jnp.int32)
    last_e = ue[jnp.maximum(total - 1, 0)]
    ue = jnp.where(active, ue, last_e).astype(jnp.int32)
    local_row = jnp.arange(n_rows, dtype=jnp.int32) - start[e_sorted]
    pos_sorted = (ufirst[e_sorted] + local_row // MOE_UNIT) * MOE_UNIT + local_row % MOE_UNIT
    pos = jnp.zeros((n_rows,), jnp.int32).at[order].set(pos_sorted.astype(jnp.int32))

    ys = _moe_experts(h, tok_sorted, ue, us, un, w_gate, w_up, w_down)
    return _moe_combine(ys, pos, wts, x, final_g, normalize)


def _mixer(x, shift0, wkv0, past, lw):
    b, t, d = x.shape
    rw_proj = lw["mu"].shape[0]
    rww = lw["w_up_a"].shape[0]
    sbw = lw["w_up_b"].shape[0]
    heads = lw["sb_bias"].shape[0]
    hd = sbw // heads
    q_off, k_off, v_off = rw_proj, rw_proj + sbw, rw_proj + 2 * sbw
    ga_off, gb_off = rw_proj + 3 * sbw, rw_proj + 3 * sbw + d
    x2 = x.reshape(b * t, d)

    h = _rmsnorm(x2, lw["norm1_g"], BF16)
    proj = _matmul(h, lw["w_in"], F32, 1024, 768)
    proj3 = proj.reshape(b, t, -1)

    r, lgw, k2, v, na, bb, bonus = _rwkv_prep(proj3, shift0, lw["mu"], lw["w0"], lw["w2"], lw["a0"],
                                              lw["a2"], lw["kk"], lw["ka"], lw["rk"])
    o_a, wkv_new = _rwkv_recurrence(r, lgw, k2, v, na, bb, bonus, wkv0, lw["ln_g"], lw["ln_b"])
    shift_new = proj3[:, -1, :rw_proj]

    k_new = proj3[:, :, k_off:k_off + sbw]
    v_new = proj3[:, :, v_off:v_off + sbw]
    if past is None:
        o_b = _sb_prompt(proj3, lw["sb_bias"], q_off, k_off, v_off, heads, hd)
    else:
        cache_k, cache_v, layer, page_table = past
        q = proj3[:, :, q_off:q_off + sbw]
        o_b = _sb_sample(q, k_new, v_new, cache_k, cache_v, layer, page_table, lw["sb_bias"]).astype(BF16)

    mixed = _merge_up(o_a.reshape(b * t, rww), o_b.reshape(b * t, sbw), proj, lw["w_up_a"], lw["w_up_b"],
                      ga_off, gb_off)
    x1 = _out_proj(mixed, lw["w_out"], x2)
    return (x1, k_new.reshape(b, t, heads, hd), v_new.reshape(b, t, heads, hd), wkv_new, shift_new)


def kernel(x_prompt, x_sample, cache_k, cache_v, state_wkv, state_shift, page_table, norm1_g, w_in, rwkv_mu, rwkv_w0, rwkv_w2, rwkv_a0, rwkv_a2, rwkv_kk, rwkv_ka, rwkv_rk, rwkv_ln_g, rwkv_ln_b, sb_bias, w_up_a, w_up_b, w_out, norm2_g, router_group_w, router_group_b, router_expert_w, router_expert_b, expert_w_gate, expert_w_up, expert_w_down, normf_g):
    depth = w_in.shape[0]
    bp, tp, d = x_prompt.shape
    bs, ts, _ = x_sample.shape
    rw_proj = rwkv_mu.shape[1]
    rww = w_up_a.shape[1]
    rw_heads = rww // RW_HEAD_DIM
    xp, xs = x_prompt, x_sample
    outs = [[] for _ in range(8)]
    for l in range(depth):
        lw = dict(norm1_g=norm1_g[l], w_in=w_in[l].astype(BF16), mu=rwkv_mu[l], w0=rwkv_w0[l],
                  w2=rwkv_w2[l], a0=rwkv_a0[l], a2=rwkv_a2[l], kk=rwkv_kk[l], ka=rwkv_ka[l],
                  rk=rwkv_rk[l].reshape(-1), ln_g=rwkv_ln_g[l], ln_b=rwkv_ln_b[l], sb_bias=sb_bias[l],
                  w_up_a=w_up_a[l].astype(BF16), w_up_b=w_up_b[l].astype(BF16),
                  w_out=w_out[l].astype(BF16))
        shift0 = jnp.zeros((bp, rw_proj), F32)
        wkv0 = jnp.zeros((bp, rw_heads, RW_HEAD_DIM, RW_HEAD_DIM), F32)
        res_p = _mixer(xp, shift0, wkv0, None, lw)
        res_s = _mixer(xs, state_shift[l], state_wkv[l], (cache_k, cache_v, l, page_table), lw)
        for i in range(4):
            outs[i].append(res_p[1 + i])
            outs[4 + i].append(res_s[1 + i])
        x1 = jnp.concatenate([res_p[0], res_s[0]], axis=0)
        last = l == depth - 1
        x2 = _moe_layer(x1, norm2_g[l], router_group_w[l], router_group_b[l], router_expert_w[l],
                        router_expert_b[l], expert_w_gate[l], expert_w_up[l], expert_w_down[l],
                        normf_g, last)
        xp = x2[:bp * tp].reshape(bp, tp, d)
        xs = x2[bp * tp:].reshape(bs, ts, d)
    return (xp, xs) + tuple(jnp.stack(o) for o in outs)
```

```python
import functools

import jax
import jax.numpy as jnp
from jax import lax
from jax.experimental import pallas as pl
from jax.experimental.pallas import tpu as pltpu

F32 = jnp.float32
BF16 = jnp.bfloat16

LANES = 128
RW_HEAD_DIM = 64
RW_CHUNK = 64
RW_INV_BLOCK = 16
RW_PAIRS_PER_STEP = 8
RW_GN_EPS = 64e-5
RMS_EPS = 1e-6
TOP_K = 2
MOE_SUB = 128
MOE_UNIT = 3 * MOE_SUB
MOE_DOWN_COLS = 512
VMEM_LIMIT_MB = 56


def _params(sem, vmem_mb=VMEM_LIMIT_MB):
    return pltpu.CompilerParams(dimension_semantics=sem, vmem_limit_bytes=vmem_mb << 20)


def _tile(n, pref, mult=8):
    t = min(pref, n)
    while t >= mult:
        if n % t == 0 and t % mult == 0:
            return t
        t -= 1
    return n


def _dg(a, b, nt=False):
    if a.ndim == 3:
        dims = (((2,), (2 if nt else 1,)), ((0,), (0,)))
    else:
        dims = (((1,), (1 if nt else 0,)), ((), ()))
    return lax.dot_general(a, b, dims, preferred_element_type=F32)


def _split(x):
    hi = x.astype(BF16)
    lo = (x - hi.astype(F32)).astype(BF16)
    return hi, lo


def _dot1(a, b, nt=False):
    return _dg(a.astype(BF16), b.astype(BF16), nt)


def _dot3(a, b, nt=False):
    ah, al = _split(a)
    bh, bl = _split(b)
    return _dg(ah, bh, nt) + (_dg(ah, bl, nt) + _dg(al, bh, nt))


def _dot2x(a, b_exact):
    ah, al = _split(a)
    return _dg(ah, b_exact) + _dg(al, b_exact)


def _sigmoid(x):
    return 1.0 / (1.0 + jnp.exp(-x))


def _softplus(x):
    return jnp.maximum(x, 0.0) + jnp.log1p(jnp.exp(-jnp.abs(x)))


def _softplus_pair(z):
    l = jnp.log(1.0 + jnp.exp(-jnp.abs(z)))
    return jnp.maximum(z, 0.0) + l, jnp.minimum(z, 0.0) - l


def _rmsnorm_kernel(x_ref, g_ref, o_ref):
    x = x_ref[...]
    ms = jnp.mean(x * x, axis=-1, keepdims=True)
    o_ref[...] = (x * lax.rsqrt(ms + RMS_EPS) * g_ref[...]).astype(o_ref.dtype)


def _rmsnorm(x, g, out_dtype):
    m, d = x.shape
    tm = _tile(m, 256)
    return pl.pallas_call(
        _rmsnorm_kernel,
        grid=(m // tm,),
        in_specs=[pl.BlockSpec((tm, d), lambda i: (i, 0)),
                  pl.BlockSpec((1, d), lambda i: (0, 0))],
        out_specs=pl.BlockSpec((tm, d), lambda i: (i, 0)),
        out_shape=jax.ShapeDtypeStruct((m, d), out_dtype),
        compiler_params=_params(("parallel",)),
        name="rmsnorm",
    )(x, g.reshape(1, d))


def _mm_kernel(a_ref, b_ref, o_ref):
    o_ref[...] = jnp.dot(a_ref[...], b_ref[...], preferred_element_type=F32).astype(o_ref.dtype)


def _matmul(a, b, out_dtype, tm_pref, tn_pref):
    m, k = a.shape
    n = b.shape[1]
    tm = _tile(m, tm_pref, 16)
    tn = _tile(n, tn_pref, LANES)
    return pl.pallas_call(
        _mm_kernel,
        grid=(m // tm, n // tn),
        in_specs=[pl.BlockSpec((tm, k), lambda i, j: (i, 0)),
                  pl.BlockSpec((k, tn), lambda i, j: (0, j))],
        out_specs=pl.BlockSpec((tm, tn), lambda i, j: (i, j)),
        out_shape=jax.ShapeDtypeStruct((m, n), out_dtype),
        compiler_params=_params(("parallel", "arbitrary")),
        name="in_proj",
    )(a, b)


def _merge_kernel(oa_ref, ob_ref, wa_ref, wb_ref, ga_ref, gb_ref, o_ref):
    ya = jnp.dot(oa_ref[...], wa_ref[...], preferred_element_type=F32)
    yb = jnp.dot(ob_ref[...], wb_ref[...], preferred_element_type=F32)
    o_ref[...] = (_sigmoid(ga_ref[...]) * ya + _sigmoid(gb_ref[...]) * yb).astype(o_ref.dtype)


def _merge_up(o_a, o_b, proj, w_up_a, w_up_b, ga_off, gb_off):
    m, ka = o_a.shape
    kb = o_b.shape[1]
    d = w_up_a.shape[1]
    tm = _tile(m, 1024, 16)
    tn = 256
    assert d % tn == 0 and ga_off % tn == 0 and gb_off % tn == 0
    ga_blk, gb_blk = ga_off // tn, gb_off // tn
    return pl.pallas_call(
        _merge_kernel,
        grid=(m // tm, d // tn),
        in_specs=[pl.BlockSpec((tm, ka), lambda i, j: (i, 0)),
                  pl.BlockSpec((tm, kb), lambda i, j: (i, 0)),
                  pl.BlockSpec((ka, tn), lambda i, j: (0, j)),
                  pl.BlockSpec((kb, tn), lambda i, j: (0, j)),
                  pl.BlockSpec((tm, tn), lambda i, j: (i, ga_blk + j)),
                  pl.BlockSpec((tm, tn), lambda i, j: (i, gb_blk + j))],
        out_specs=pl.BlockSpec((tm, tn), lambda i, j: (i, j)),
        out_shape=jax.ShapeDtypeStruct((m, d), BF16),
        compiler_params=_params(("parallel", "arbitrary")),
        name="merge_up",
    )(o_a, o_b, w_up_a, w_up_b, proj, proj)


def _outproj_kernel(a_ref, b_ref, x_ref, o_ref):
    o_ref[...] = x_ref[...] + jnp.dot(a_ref[...], b_ref[...], preferred_element_type=F32)


def _out_proj(mixed, w_out, x):
    m, k = mixed.shape
    n = w_out.shape[1]
    tm = _tile(m, 1024, 16)
    tn = _tile(n, 512, LANES)
    return pl.pallas_call(
        _outproj_kernel,
        grid=(m // tm, n // tn),
        in_specs=[pl.BlockSpec((tm, k), lambda i, j: (i, 0)),
                  pl.BlockSpec((k, tn), lambda i, j: (0, j)),
                  pl.BlockSpec((tm, tn), lambda i, j: (i, j))],
        out_specs=pl.BlockSpec((tm, tn), lambda i, j: (i, j)),
        out_shape=jax.ShapeDtypeStruct((m, n), F32),
        compiler_params=_params(("parallel", "arbitrary")),
        name="out_proj",
    )(mixed, w_out, x)


def _rwkv_prep_kernel(p_ref, sh_ref, mu_ref, w0_ref, w2_ref, a0_ref, a2_ref, kkw_ref, kaw_ref,
                      rkw_ref, ones_ref,
                      r_out, lw_out, k_out, v_out, na_out, bb_out, bon_out,
                      carry, *, rww, dl, al):
    t = pl.program_id(1)
    tt = p_ref.shape[1]

    @pl.when(t == 0)
    def _():
        carry[0:1, :] = sh_ref[0]

    row0 = lax.broadcasted_iota(jnp.int32, (tt, 1), 0) == 0

    def mixed(lo, hi):
        p = p_ref[0, :, lo:hi]
        prev = jnp.where(row0, carry[0:1, lo:hi], pltpu.roll(p, 1, 0))
        return p + (prev - p) * mu_ref[:, lo:hi]

    wd = mixed(3 * rww, 3 * rww + dl)
    ad = mixed(3 * rww + dl, 3 * rww + dl + al)
    w_lora = _dot3(jnp.tanh(wd), w2_ref[...])
    a_lora = _dot3(ad, a2_ref[...])
    ones = ones_ref[...]
    for s in range(rww // LANES):
        c0, c1 = s * LANES, (s + 1) * LANES
        r = mixed(c0, c1)
        k = mixed(rww + c0, rww + c1)
        v = mixed(2 * rww + c0, 2 * rww + c1)
        w = w0_ref[:, c0:c1] + w_lora[:, c0:c1]
        lw = -jnp.exp(-_softplus(-w) - 0.5)
        a = _sigmoid(a0_ref[:, c0:c1] + a_lora[:, c0:c1])
        kk = k * kkw_ref[:, c0:c1]
        ss = _dot2x(kk * kk, ones)
        kk = kk / jnp.maximum(jnp.sqrt(ss), 1e-12)
        k2 = k * (1.0 + (a - 1.0) * kaw_ref[:, c0:c1])
        rk = _dot2x(r * k2 * rkw_ref[:, c0:c1], ones)
        r_out[0, :, c0:c1] = r
        lw_out[0, :, c0:c1] = lw
        k_out[0, :, c0:c1] = k2
        v_out[0, :, c0:c1] = v
        na_out[0, :, c0:c1] = -kk
        bb_out[0, :, c0:c1] = kk * a
        bon_out[0, :, c0:c1] = rk * v
    carry[0:1, :] = p_ref[0, tt - 1:tt, :]


def _head_ones():
    i = jnp.arange(LANES) // RW_HEAD_DIM
    return (i[:, None] == i[None, :]).astype(BF16)


def _rwkv_prep(proj, shift0, mu, w0, w2, a0, a2, kkw, kaw, rkw):
    b, t, _ = proj.shape
    p = mu.shape[0]
    rww = w0.shape[0]
    dl, al = w2.shape[0], a2.shape[0]
    tt = _tile(t, 128)
    row = lambda x: x.reshape(1, -1)
    out = jax.ShapeDtypeStruct((b, t, rww), F32)
    vec = lambda n: pl.BlockSpec((1, n), lambda i, j: (0, 0))
    full = lambda s: pl.BlockSpec(s, lambda i, j: (0, 0))
    blk = pl.BlockSpec((1, tt, rww), lambda i, j: (i, j, 0))
    return pl.pallas_call(
        functools.partial(_rwkv_prep_kernel, rww=rww, dl=dl, al=al),
        grid=(b, t // tt),
        in_specs=[pl.BlockSpec((1, tt, p), lambda i, j: (i, j, 0)),
                  pl.BlockSpec((1, 1, p), lambda i, j: (i, 0, 0)),
                  vec(p), vec(rww), full((dl, rww)), vec(rww), full((al, rww)),
                  vec(rww), vec(rww), vec(rww), full((LANES, LANES))],
        out_specs=[blk] * 7,
        out_shape=[out] * 7,
        scratch_shapes=[pltpu.VMEM((8, p), F32)],
        compiler_params=_params(("parallel", "arbitrary")),
        name="rwkv_prep",
    )(proj, shift0.reshape(b, 1, p), row(mu), row(w0), w2, row(a0), a2, row(kkw), row(kaw),
      row(rkw), _head_ones())


def _tri_inv(a, n, nb):
    row = lax.broadcasted_iota(jnp.int32, (n, n), 0)
    col = lax.broadcasted_iota(jnp.int32, (n, n), 1)
    dmask = (row // RW_INV_BLOCK) == (col // RW_INV_BLOCK)
    eye = jnp.where(row == col, 1.0, 0.0)
    ad = jnp.where(dmask, a, 0.0)
    ao = jnp.where(dmask, 0.0, a)
    dinv = eye + ad
    p = ad
    k = 1
    while 2 * k < RW_INV_BLOCK:
        p = _dot3(p, p)
        dinv = dinv + _dot3(dinv, p)
        k *= 2
    bm = _dot3(dinv, ao)
    x = eye + bm
    p = bm
    k = 1
    while 2 * k < nb:
        p = _dot3(p, p)
        x = x + _dot3(x, p)
        k *= 2
    return _dot3(x, dinv)


def _rwkv_chunk_math(r, cum, lw, kk, v, na, bb, s):
    ch = r.shape[1]
    n = 2 * ch
    cl = cum[:, ch - 1:ch, :]
    e_neg = jnp.exp(-cum)
    e_end = jnp.exp(cl - cum)
    a_t = na * jnp.exp(cum - lw)
    r_t = r * jnp.exp(cum)
    lane_lo = lax.broadcasted_iota(jnp.int32, (ch, LANES), 1) < RW_HEAD_DIM

    def hat(x):
        return jnp.concatenate([jnp.where(lane_lo, x, 0.0), jnp.where(lane_lo, 0.0, x)], axis=1)

    ath, rth, vh = hat(a_t), hat(r_t), hat(v)
    btkt = jnp.concatenate([hat(bb * e_neg), hat(kk * e_neg)], axis=1)
    bbh, kbh = hat(bb * e_end), hat(kk * e_end)

    row = lax.broadcasted_iota(jnp.int32, (n, n), 0)
    col = lax.broadcasted_iota(jnp.int32, (n, n), 1)
    strict = row > col
    incl = row >= col
    sc_a = _dot3(ath, btkt, nt=True)
    sc_r = _dot1(rth, btkt, nt=True)
    a_ab = jnp.where(strict, sc_a[:, :, :n], 0.0)
    a_ak = jnp.where(strict, sc_a[:, :, n:], 0.0)
    a_rb = jnp.where(incl, sc_r[:, :, :n], 0.0)
    a_rk = jnp.where(incl, sc_r[:, :, n:], 0.0)

    tinv = _tri_inv(a_ab, n, ch // RW_INV_BLOCK)
    x = _dot3(tinv, jnp.concatenate([_dot1(a_ak, vh), ath], axis=2))
    u0, a2 = x[:, :, :LANES], x[:, :, LANES:]
    z = _dot1(a_rb, x)
    y0 = z[:, :, :LANES] + _dot1(a_rk, vh)
    rh = rth + z[:, :, LANES:]
    yh = _dot1(rh, s, nt=True) + y0
    y = yh[:, :ch] + yh[:, ch:]

    tr = lambda t: jnp.swapaxes(t, 1, 2)
    m = jnp.where(row == col, jnp.exp(cl), 0.0) + _dot3(tr(a2), bbh)
    nn = _dot1(jnp.concatenate([tr(u0), tr(vh)], axis=2), jnp.concatenate([bbh, kbh], axis=1))
    return y, _dot3(s, m) + nn


def _rwkv_chunk_kernel(r_ref, lw_ref, k_ref, v_ref, na_ref, bb_ref, bon_ref, lng_ref, lnb_ref,
                       s0_ref, ltri_ref, ones_ref, o_ref, sf_ref, s_scr):
    c = pl.program_id(2)
    pairs = s_scr.shape[0]
    ch = r_ref.shape[1]

    @pl.when(c == 0)
    def _():
        s_scr[...] = s0_ref[0]

    def pairwise(x):
        return jnp.stack([x[:, g * LANES:(g + 1) * LANES] for g in range(pairs)], axis=0)

    lw = lw_ref[0]
    cum = _dot2x_lhs(ltri_ref[...], lw)
    y, s_new = _rwkv_chunk_math(pairwise(r_ref[0]), pairwise(cum), pairwise(lw), pairwise(k_ref[0]),
                                pairwise(v_ref[0]), pairwise(na_ref[0]), pairwise(bb_ref[0]), s_scr[...])
    s_scr[...] = s_new
    sf_ref[0] = s_new

    ones = ones_ref[...]
    y2 = y.reshape(pairs * ch, LANES)
    mean = _dot2x(y2, ones) * (1.0 / RW_HEAD_DIM)
    d = y2 - mean
    var = _dot2x(d * d, ones) * (1.0 / RW_HEAD_DIM)
    yn = d * lax.rsqrt(var + RW_GN_EPS)
    for g in range(pairs):
        sl = slice(g * LANES, (g + 1) * LANES)
        out = yn[g * ch:(g + 1) * ch] * lng_ref[:, sl] + lnb_ref[:, sl] + bon_ref[0, :, sl]
        o_ref[0, :, sl] = out.astype(o_ref.dtype)


def _dot2x_lhs(a_exact, b):
    bh, bl = _split(b)
    return _dg(a_exact, bh) + _dg(a_exact, bl)


def _rwkv_recurrence(r, lw, k, v, na, bb, bonus, wkv0, ln_g, ln_b):
    b, t, rww = r.shape
    hp = rww // LANES
    hd = RW_HEAD_DIM
    ch = RW_CHUNK
    tp = -(-t // ch) * ch
    if tp != t:
        pad = lambda x: jnp.pad(x, ((0, 0), (0, tp - t), (0, 0)))
        r, lw, k, v, na, bb, bonus = map(pad, (r, lw, k, v, na, bb, bonus))
    nc = tp // ch
    w0 = wkv0.astype(F32).reshape(b, hp, 2, hd, hd)
    zero = jnp.zeros((b, hp, hd, hd), F32)
    s0 = jnp.concatenate([jnp.concatenate([w0[:, :, 0], zero], axis=-1),
                          jnp.concatenate([zero, w0[:, :, 1]], axis=-1)], axis=-2)
    ltri = (jnp.arange(ch)[:, None] >= jnp.arange(ch)[None, :]).astype(BF16)
    pairs = _tile(hp, RW_PAIRS_PER_STEP, 1)
    blk = pl.BlockSpec((1, ch, pairs * LANES), lambda i, j, c: (i, c, j))
    vec = pl.BlockSpec((1, pairs * LANES), lambda i, j, c: (0, j))
    st = pl.BlockSpec((1, pairs, LANES, LANES), lambda i, j, c: (i, j, 0, 0))
    full = lambda s: pl.BlockSpec(s, lambda i, j, c: (0, 0))
    o, sf = pl.pallas_call(
        _rwkv_chunk_kernel,
        grid=(b, hp // pairs, nc),
        in_specs=[blk] * 7 + [vec, vec, st, full((ch, ch)), full((LANES, LANES))],
        out_specs=[blk, st],
        out_shape=[jax.ShapeDtypeStruct((b, tp, rww), BF16),
                   jax.ShapeDtypeStruct((b, hp, LANES, LANES), F32)],
        scratch_shapes=[pltpu.VMEM((pairs, LANES, LANES), F32)],
        compiler_params=_params(("parallel", "parallel", "arbitrary")),
        name="rwkv_chunk",
    )(r, lw, k, v, na, bb, bonus, ln_g.reshape(1, rww), ln_b.reshape(1, rww), s0, ltri, _head_ones())
    sf = sf.reshape(b, hp, 2, hd, 2, hd)
    wkv = jnp.stack([sf[:, :, 0, :, 0, :], sf[:, :, 1, :, 1, :]], axis=2).reshape(b, 2 * hp, hd, hd)
    return o[:, :t], wkv


def _suffix_matrix(tk):
    i = jnp.arange(tk)
    upper = (i[:, None] > i[None, :]).astype(BF16)
    return jnp.concatenate([upper, jnp.ones((tk, tk), BF16)], axis=1)


def _sb_block(q, k, v, bias, mxo, mask):
    tk = k.shape[0]
    hd = v.shape[1]
    sp, la = _softplus_pair(_dg(q, k, nt=True) + bias)
    lsm = -sp if mask is None else jnp.where(mask, -sp, 0.0)
    suf = _dg(lsm.astype(BF16), mxo)
    att = jnp.exp(la + suf)
    if mask is not None:
        att = jnp.where(mask, att, 0.0)
    tot = jnp.broadcast_to(suf[:, 0:1] + lsm[:, 0:1], (q.shape[0], hd))
    return _dg(att.astype(BF16), v), tot


def _sb_prompt_kernel(bias_ref, q_ref, k_ref, v_ref, mxo_ref, o_ref, acc, carry, *, scale):
    h = pl.program_id(1)
    qi = pl.program_id(2)
    tq = q_ref.shape[1]
    bias = bias_ref[h]
    q = (q_ref[0] * scale).astype(BF16)
    mxo = mxo_ref[...]

    hd = q_ref.shape[2]

    def block(kb, mask=None):
        start = pl.multiple_of(kb * tq, tq)
        k = k_ref[0, pl.ds(start, tq), :].astype(BF16)
        v = v_ref[0, pl.ds(start, tq), :].astype(BF16)
        return _sb_block(q, k, v, bias, mxo, mask)

    row = lax.broadcasted_iota(jnp.int32, (tq, tq), 0)
    col = lax.broadcasted_iota(jnp.int32, (tq, tq), 1)
    pv, tot = block(qi, col < row)
    acc[...] = pv
    carry[...] = tot

    def pair(i, c):
        kb = qi - 1 - 2 * i
        start = pl.multiple_of((kb - 1) * tq, tq)
        k2 = k_ref[0, pl.ds(start, 2 * tq), :].astype(BF16)
        v2 = v_ref[0, pl.ds(start, 2 * tq), :].astype(BF16)
        sp, la = _softplus_pair(_dg(q, k2, nt=True) + bias)
        lsm = -sp
        lsb = lsm.astype(BF16)
        suf = _dg(jnp.concatenate([lsb[:, :tq], lsb[:, tq:]], axis=0), mxo)
        suf_old, suf_new = suf[:tq], suf[tq:]
        tot_new = suf_new[:, 0:1] + lsm[:, tq:tq + 1]
        tot_old = suf_old[:, 0:1] + lsm[:, 0:1]
        att_old = jnp.exp(la[:, :tq] + suf_old + tot_new)
        att_new = jnp.exp(la[:, tq:] + suf_new)
        pv2 = _dg(jnp.concatenate([att_old, att_new], axis=1).astype(BF16), v2)
        cr = carry[...]
        acc[...] += jnp.exp(cr) * pv2
        carry[...] = cr + jnp.broadcast_to(tot_old + tot_new, (tq, hd))
        return c

    lax.fori_loop(0, qi // 2, pair, 0)

    @pl.when(qi % 2 == 1)
    def _():
        pv0, _ = block(0)
        acc[...] += jnp.exp(carry[...]) * pv0

    o_ref[0] = acc[...].astype(o_ref.dtype)


def _sb_prompt(proj, bias, q_off, k_off, v_off, heads, hd):
    b, t, _ = proj.shape
    tq = _tile(t, 256, LANES)
    qb, kb, vb = q_off // hd, k_off // hd, v_off // hd
    return pl.pallas_call(
        functools.partial(_sb_prompt_kernel, scale=hd ** -0.5),
        grid=(b, heads, t // tq),
        in_specs=[pl.BlockSpec(memory_space=pltpu.SMEM),
                  pl.BlockSpec((1, tq, hd), lambda i, h, j: (i, j, qb + h)),
                  pl.BlockSpec((1, t, hd), lambda i, h, j: (i, 0, kb + h)),
                  pl.BlockSpec((1, t, hd), lambda i, h, j: (i, 0, vb + h)),
                  pl.BlockSpec((tq, tq), lambda i, h, j: (0, 0))],
        out_specs=pl.BlockSpec((1, tq, hd), lambda i, h, j: (i, j, h)),
        out_shape=jax.ShapeDtypeStruct((b, t, heads * hd), BF16),
        scratch_shapes=[pltpu.VMEM((tq, hd), F32), pltpu.VMEM((tq, hd), F32)],
        compiler_params=_params(("parallel", "parallel", "arbitrary")),
        name="sb_prompt",
    )(bias.astype(F32), proj, proj, proj, _suffix_matrix(tq)[:, :tq])


SB_QPAD = 16
SB_PAGES_PER_STEP = 4


def _sb_sample_kernel(pt_ref, q_ref, kn_ref, vn_ref, *refs, heads, hd, tq, scale, pps):
    kp_refs, vp_refs = refs[:pps], refs[pps:2 * pps]
    bias_ref, mxo_ref, o_ref, zbuf, abuf, pvbuf, acc, carry = refs[2 * pps:]
    j = pl.program_id(1)
    nj = pl.num_programs(1)
    qp = SB_QPAD
    rows = heads * qp

    def pages(k_refs, v_refs, masked):
        n = len(k_refs)
        tk = k_refs[0].shape[1] // heads

        def head_rows(ref, h):
            return ref[0, pl.ds(h, tk, stride=heads), :].astype(BF16)

        qs = []
        for h in range(heads):
            qh = jnp.concatenate([q_ref[0, :, h * hd:(h + 1) * hd] * scale,
                                  jnp.zeros((qp - tq, hd), F32)], axis=0)
            qs.append(qh.astype(BF16))
        for s in range(n):
            for h in range(heads):
                zbuf[s, h * qp:(h + 1) * qp, :] = _dg(qs[h], head_rows(k_refs[s], h), nt=True)
        sp, la = _softplus_pair(zbuf[0:n] + bias_ref[...])
        if masked:
            qidx = lax.broadcasted_iota(jnp.int32, (rows, tk), 0) % qp
            kidx = lax.broadcasted_iota(jnp.int32, (rows, tk), 1)
            mask = kidx < qidx
            lsm = jnp.where(mask, -sp, 0.0)
        else:
            lsm = -sp
        st = _dg(lsm.astype(BF16).reshape(n * rows, tk), mxo_ref[...]).reshape(n, rows, 2 * tk)
        att = jnp.exp(la + st[:, :, :tk])
        if masked:
            att = jnp.where(mask, att, 0.0)
        abuf[0:n] = att
        for s in range(n):
            for h in range(heads):
                ah = abuf[s, h * qp:(h + 1) * qp, :].astype(BF16)
                pvbuf[s, h * qp:(h + 1) * qp, :] = _dg(ah, head_rows(v_refs[s], h))
        return st[:, :, tk:tk + hd]

    @pl.when(j == 0)
    def _():
        carry[...] = pages([kn_ref], [vn_ref], True)[0]
        acc[...] = pvbuf[0]

    @pl.when(j > 0)
    def _():
        tots = pages(kp_refs, vp_refs, False)
        cr = carry[...]
        out = acc[...]
        for s in range(pps):
            out = out + jnp.exp(cr) * pvbuf[s]
            cr = cr + tots[s]
        acc[...] = out
        carry[...] = cr

    @pl.when(j == nj - 1)
    def _():
        for h in range(heads):
            o_ref[0, :, h * hd:(h + 1) * hd] = acc[h * qp:h * qp + tq, :]


def _sb_sample(q, k_new, v_new, cache_k, cache_v, layer, page_table, bias):
    nb, tq, sbw = q.shape
    depth, nphys, page, heads, hd = cache_k.shape
    n_pages = page_table.shape[1]
    assert tq <= SB_QPAD and tq <= page and page % LANES == 0
    pad = lambda x: jnp.pad(x.reshape(nb, tq * heads, hd), ((0, 0), (0, (page - tq) * heads), (0, 0)))
    ck = cache_k.reshape(depth * nphys, page * heads, hd)
    cv = cache_v.reshape(depth * nphys, page * heads, hd)
    page_table = page_table + layer * nphys
    rows = heads * SB_QPAD
    bias_rows = jnp.broadcast_to(jnp.repeat(bias.astype(F32), SB_QPAD)[:, None], (rows, page))
    pps = _tile(n_pages, SB_PAGES_PER_STEP, 1)

    def page_map(s):
        def index(i, j, pt):
            return (pt[i * n_pages + n_pages - 1 - (jnp.maximum(j, 1) - 1) * pps - s], 0, 0)
        return index

    seq = lambda i, j, pt: (i, 0, 0)
    const = lambda i, j, pt: (0, 0)
    pages = [pl.BlockSpec((1, page * heads, hd), page_map(s)) for s in range(pps)]
    return pl.pallas_call(
        functools.partial(_sb_sample_kernel, heads=heads, hd=hd, tq=tq, scale=hd ** -0.5, pps=pps),
        grid_spec=pltpu.PrefetchScalarGridSpec(
            num_scalar_prefetch=1,
            grid=(nb, n_pages // pps + 1),
            in_specs=[pl.BlockSpec((1, tq, sbw), seq),
                      pl.BlockSpec((1, page * heads, hd), seq),
                      pl.BlockSpec((1, page * heads, hd), seq)] + pages + pages +
                     [pl.BlockSpec((rows, page), const),
                      pl.BlockSpec((page, 2 * page), const)],
            out_specs=pl.BlockSpec((1, tq, sbw), seq),
            scratch_shapes=[pltpu.VMEM((pps, rows, page), F32), pltpu.VMEM((pps, rows, page), F32),
                            pltpu.VMEM((pps, rows, hd), F32),
                            pltpu.VMEM((rows, hd), F32), pltpu.VMEM((rows, hd), F32)]),
        out_shape=jax.ShapeDtypeStruct((nb, tq, sbw), F32),
        compiler_params=_params(("parallel", "arbitrary")),
        name="sb_sample",
    )(page_table.reshape(-1).astype(jnp.int32), q, pad(k_new), pad(v_new), *([ck] * pps), *([cv] * pps),
      bias_rows, _suffix_matrix(page))


def _router_kernel(x_ref, g_ref, rw_ref, rb_ref, h_ref, id_ref, wt_ref, *, n_groups, per_group):
    x = x_ref[...]
    ms = jnp.mean(x * x, axis=-1, keepdims=True)
    h = x * lax.rsqrt(ms + RMS_EPS) * g_ref[...]
    h_ref[...] = h
    logits = _dot3(h, rw_ref[...]) + rb_ref[...]
    tm = logits.shape[0]
    lane = lax.broadcasted_iota(jnp.int32, (tm, LANES), 1)
    lane_f = lane.astype(F32)
    neg = jnp.finfo(F32).min
    far = float(LANES)
    is_g = lane < n_groups
    gl = jnp.where(is_g, logits, neg)
    gmax = jnp.max(gl, axis=1, keepdims=True)
    gidx = jnp.min(jnp.where(gl == gmax, lane_f, far), axis=1, keepdims=True).astype(jnp.int32)
    gprob = 1.0 / jnp.sum(jnp.where(is_g, jnp.exp(logits - gmax), 0.0), axis=1, keepdims=True)
    lo = n_groups + gidx * per_group
    el = jnp.where((lane >= lo) & (lane < lo + per_group), logits, neg)
    v1 = jnp.max(el, axis=1, keepdims=True)
    i1 = jnp.min(jnp.where(el == v1, lane_f, far), axis=1, keepdims=True).astype(jnp.int32)
    el2 = jnp.where(lane == i1, neg, el)
    v2 = jnp.max(el2, axis=1, keepdims=True)
    i2 = jnp.min(jnp.where((el2 == v2) & (lane != i1), lane_f, far), axis=1, keepdims=True).astype(jnp.int32)
    t = jnp.exp(v2 - v1)
    w1 = gprob / (1.0 + t)
    w2 = gprob * t / (1.0 + t)
    id_ref[...] = jnp.where(lane == 0, i1 - n_groups, jnp.where(lane == 1, i2 - n_groups, 0))
    wt_ref[...] = jnp.where(lane == 0, w1, jnp.where(lane == 1, w2, 0.0))


def _router(x, g, rg_w, rg_b, re_w, re_b):
    m, d = x.shape
    n_groups = rg_w.shape[1]
    n_exp = re_w.shape[1]
    assert n_groups + n_exp <= LANES
    padc = LANES - n_groups - n_exp
    rw = jnp.concatenate([rg_w, re_w, jnp.zeros((d, padc), F32)], axis=1).astype(F32)
    rb = jnp.concatenate([rg_b, re_b, jnp.zeros((padc,), F32)]).reshape(1, LANES).astype(F32)
    tm = _tile(m, 256)
    row = pl.BlockSpec((tm, d), lambda i: (i, 0))
    nar = pl.BlockSpec((tm, LANES), lambda i: (i, 0))
    h, ids, wts = pl.pallas_call(
        functools.partial(_router_kernel, n_groups=n_groups, per_group=n_exp // n_groups),
        grid=(m // tm,),
        in_specs=[row, pl.BlockSpec((1, d), lambda i: (0, 0)),
                  pl.BlockSpec((d, LANES), lambda i: (0, 0)),
                  pl.BlockSpec((1, LANES), lambda i: (0, 0))],
        out_specs=[row, nar, nar],
        out_shape=[jax.ShapeDtypeStruct((m, d), F32),
                   jax.ShapeDtypeStruct((m, LANES), jnp.int32),
                   jax.ShapeDtypeStruct((m, LANES), F32)],
        compiler_params=_params(("parallel",)),
        name="router",
    )(x, g.reshape(1, d), rw, rb)
    return h, ids[:, :TOP_K], wts


def _moe_kernel(ue_ref, us_ref, un_ref, tok_ref, h_hbm, wg_ref, wu_ref, wd_ref, y_ref,
                xf, xb, wgb, wub, wdb, sem):
    u = pl.program_id(0)
    f = pl.program_id(1)
    n = un_ref[u]

    def row_copy(tok, i):
        return pltpu.make_async_copy(h_hbm.at[pl.ds(tok, 1), :], xf.at[pl.ds(i, 1), :], sem)

    @pl.when((f == 0) & (n > 0))
    def _():
        xf[...] = jnp.zeros_like(xf)
        base = us_ref[u]

        def start(i, c):
            row_copy(tok_ref[base + i], i).start()
            return c

        def wait(i, c):
            row_copy(0, i).wait()
            return c

        lax.fori_loop(0, n, start, 0)
        lax.fori_loop(0, n, wait, 0)
        xb[...] = xf[...].astype(BF16)

    @pl.when((f == 0) & (n == 0))
    def _():
        y_ref[...] = jnp.zeros_like(y_ref)

    @pl.when(n > 0)
    def _():
        wgb[...] = wg_ref[0].astype(BF16)
        wub[...] = wu_ref[0].astype(BF16)
        wdb[...] = wd_ref[0].astype(BF16)
        n_sub = (n + (MOE_SUB - 1)) // MOE_SUB
        for k in range(1, MOE_UNIT // MOE_SUB + 1):
            m_rows = k * MOE_SUB

            @pl.when(n_sub == k)
            def _():
                x = xb[0:m_rows, :]
                g = _dg(x, wgb[...])
                up = _dg(x, wub[...])
                hid = (g * _sigmoid(g) * up).astype(BF16)
                d = y_ref.shape[1]
                tn = _tile(d, MOE_DOWN_COLS, LANES)

                @pl.when(f == 0)
                def _():
                    for c0 in range(0, d, tn):
                        y_ref[0:m_rows, c0:c0 + tn] = _dg(hid, wdb[:, c0:c0 + tn])
                    if m_rows < MOE_UNIT:
                        y_ref[m_rows:MOE_UNIT, :] = jnp.zeros((MOE_UNIT - m_rows, d), F32)

                @pl.when(f > 0)
                def _():
                    for c0 in range(0, d, tn):
                        y_ref[0:m_rows, c0:c0 + tn] += _dg(hid, wdb[:, c0:c0 + tn])


def _moe_experts(h, tok_sorted, unit_e, unit_start, unit_n, w_gate, w_up, w_down):
    m, d = h.shape
    n_units = unit_e.shape[0]
    de = w_gate.shape[2]
    tf = _tile(de, 256, LANES)
    nf = de // tf

    def fidx(u, f, un):
        return jnp.where(un[u] > 0, f, nf - 1)

    return pl.pallas_call(
        _moe_kernel,
        grid_spec=pltpu.PrefetchScalarGridSpec(
            num_scalar_prefetch=4,
            grid=(n_units, nf),
            in_specs=[pl.BlockSpec(memory_space=pl.ANY),
                      pl.BlockSpec((1, d, tf), lambda u, f, ue, us, un, tok: (ue[u], 0, fidx(u, f, un))),
                      pl.BlockSpec((1, d, tf), lambda u, f, ue, us, un, tok: (ue[u], 0, fidx(u, f, un))),
                      pl.BlockSpec((1, tf, d), lambda u, f, ue, us, un, tok: (ue[u], fidx(u, f, un), 0))],
            out_specs=pl.BlockSpec((MOE_UNIT, d), lambda u, f, ue, us, un, tok: (u, 0)),
            scratch_shapes=[pltpu.VMEM((MOE_UNIT, d), F32), pltpu.VMEM((MOE_UNIT, d), BF16),
                            pltpu.VMEM((d, tf), BF16), pltpu.VMEM((d, tf), BF16),
                            pltpu.VMEM((tf, d), BF16), pltpu.SemaphoreType.DMA(())]),
        out_shape=jax.ShapeDtypeStruct((n_units * MOE_UNIT, d), F32),
        compiler_params=_params(("arbitrary", "arbitrary")),
        name="moe_experts",
    )(unit_e, unit_start, unit_n, tok_sorted, h, w_gate, w_up, w_down)


def _combine_kernel(pos_ref, ys_hbm, w_ref, x_ref, g_ref, o_ref, buf, sem, *, normalize):
    i = pl.program_id(0)
    tc = x_ref.shape[0]
    base = i * tc * TOP_K

    def row_copy(p, s, t):
        return pltpu.make_async_copy(ys_hbm.at[pl.ds(p, 1), :], buf.at[s, pl.ds(t, 1), :], sem)

    def start(t, c):
        for s in range(TOP_K):
            row_copy(pos_ref[base + TOP_K * t + s], s, t).start()
        return c

    def wait(t, c):
        for s in range(TOP_K):
            row_copy(0, s, t).wait()
        return c

    lax.fori_loop(0, tc, start, 0)
    lax.fori_loop(0, tc, wait, 0)
    w = w_ref[...]
    x = x_ref[...] + (w[:, 0:1] * buf[0] + w[:, 1:2] * buf[1])
    if normalize:
        ms = jnp.mean(x * x, axis=-1, keepdims=True)
        x = x * lax.rsqrt(ms + RMS_EPS) * g_ref[...]
    o_ref[...] = x


def _moe_combine(ys, pos, wts, x, g, normalize):
    m, d = x.shape
    tc = _tile(m, 128)
    return pl.pallas_call(
        functools.partial(_combine_kernel, normalize=normalize),
        grid_spec=pltpu.PrefetchScalarGridSpec(
            num_scalar_prefetch=1,
            grid=(m // tc,),
            in_specs=[pl.BlockSpec(memory_space=pl.ANY),
                      pl.BlockSpec((tc, LANES), lambda i, p: (i, 0)),
                      pl.BlockSpec((tc, d), lambda i, p: (i, 0)),
                      pl.BlockSpec((1, d), lambda i, p: (0, 0))],
            out_specs=pl.BlockSpec((tc, d), lambda i, p: (i, 0)),
            scratch_shapes=[pltpu.VMEM((TOP_K, tc, d), F32), pltpu.SemaphoreType.DMA(())]),
        out_shape=jax.ShapeDtypeStruct((m, d), F32),
        compiler_params=_params(("arbitrary",)),
        name="moe_combine",
    )(pos, ys, wts, x, g.reshape(1, d))


def _moe_layer(x, norm_g, rg_w, rg_b, re_w, re_b, w_gate, w_up, w_down, final_g, normalize):
    m, d = x.shape
    n_exp = re_w.shape[1]
    h, expert, wts = _router(x, norm_g, rg_w, rg_b, re_w, re_b)

    n_rows = m * TOP_K
    flat_e = expert.reshape(-1)
    order = jnp.argsort(flat_e).astype(jnp.int32)
    e_sorted = flat_e[order]
    tok_sorted = order // TOP_K
    counts = jnp.zeros((n_exp,), jnp.int32).at[flat_e].add(1)
    start = jnp.cumsum(counts) - counts
    units_e = (counts + MOE_UNIT - 1) // MOE_UNIT
    ucum = jnp.cumsum(units_e)
    ufirst = ucum - units_e
    n_units = n_exp + -(-n_rows // MOE_UNIT)
    uid = jnp.arange(n_units, dtype=jnp.int32)
    total = ucum[-1]
    ue = jnp.minimum(jnp.searchsorted(ucum, uid, side="right"), n_exp - 1).astype(jnp.int32)
    local = uid - ufirst[ue]
    active = uid < total
    un = jnp.where(active, jnp.clip(counts[ue] - local * MOE_UNIT, 0, MOE_UNIT), 0).astype(jnp.int32)
    us = jnp.where(active, start[ue] + local * MOE_UNIT, 0).astype(jnp.int32)
    last_e = ue[jnp.maximum(total - 1, 0)]
    ue = jnp.where(active, ue, last_e).astype(jnp.int32)
    local_row = jnp.arange(n_rows, dtype=jnp.int32) - start[e_sorted]
    pos_sorted = (ufirst[e_sorted] + local_row // MOE_UNIT) * MOE_UNIT + local_row % MOE_UNIT
    pos = jnp.zeros((n_rows,), jnp.int32).at[order].set(pos_sorted.astype(jnp.int32))

    ys = _moe_experts(h, tok_sorted, ue, us, un, w_gate, w_up, w_down)
    return _moe_combine(ys, pos, wts, x, final_g, normalize)


def _mixer(x, shift0, wkv0, past, lw):
    b, t, d = x.shape
    rw_proj = lw["mu"].shape[0]
    rww = lw["w_up_a"].shape[0]
    sbw = lw["w_up_b"].shape[0]
    heads = lw["sb_bias"].shape[0]
    hd = sbw // heads
    q_off, k_off, v_off = rw_proj, rw_proj + sbw, rw_proj + 2 * sbw
    ga_off, gb_off = rw_proj + 3 * sbw, rw_proj + 3 * sbw + d
    x2 = x.reshape(b * t, d)

    h = _rmsnorm(x2, lw["norm1_g"], BF16)
    proj = _matmul(h, lw["w_in"], F32, 1024, 768)
    proj3 = proj.reshape(b, t, -1)

    r, lgw, k2, v, na, bb, bonus = _rwkv_prep(proj3, shift0, lw["mu"], lw["w0"], lw["w2"], lw["a0"],
                                              lw["a2"], lw["kk"], lw["ka"], lw["rk"])
    o_a, wkv_new = _rwkv_recurrence(r, lgw, k2, v, na, bb, bonus, wkv0, lw["ln_g"], lw["ln_b"])
    shift_new = proj3[:, -1, :rw_proj]

    k_new = proj3[:, :, k_off:k_off + sbw]
    v_new = proj3[:, :, v_off:v_off + sbw]
    if past is None:
        o_b = _sb_prompt(proj3, lw["sb_bias"], q_off, k_off, v_off, heads, hd)
    else:
        cache_k, cache_v, layer, page_table = past
        q = proj3[:, :, q_off:q_off + sbw]
        o_b = _sb_sample(q, k_new, v_new, cache_k, cache_v, layer, page_table, lw["sb_bias"]).astype(BF16)

    mixed = _merge_up(o_a.reshape(b * t, rww), o_b.reshape(b * t, sbw), proj, lw["w_up_a"], lw["w_up_b"],
                      ga_off, gb_off)
    x1 = _out_proj(mixed, lw["w_out"], x2)
    return (x1, k_new.reshape(b, t, heads, hd), v_new.reshape(b, t, heads, hd), wkv_new, shift_new)


def kernel(x_prompt, x_sample, cache_k, cache_v, state_wkv, state_shift, page_table, norm1_g, w_in, rwkv_mu, rwkv_w0, rwkv_w2, rwkv_a0, rwkv_a2, rwkv_kk, rwkv_ka, rwkv_rk, rwkv_ln_g, rwkv_ln_b, sb_bias, w_up_a, w_up_b, w_out, norm2_g, router_group_w, router_group_b, router_expert_w, router_expert_b, expert_w_gate, expert_w_up, expert_w_down, normf_g):
    depth = w_in.shape[0]
    bp, tp, d = x_prompt.shape
    bs, ts, _ = x_sample.shape
    rw_proj = rwkv_mu.shape[1]
    rww = w_up_a.shape[1]
    rw_heads = rww // RW_HEAD_DIM
    xp, xs = x_prompt, x_sample
    outs = [[] for _ in range(8)]
    for l in range(depth):
        lw = dict(norm1_g=norm1_g[l], w_in=w_in[l].astype(BF16), mu=rwkv_mu[l], w0=rwkv_w0[l],
                  w2=rwkv_w2[l], a0=rwkv_a0[l], a2=rwkv_a2[l], kk=rwkv_kk[l], ka=rwkv_ka[l],
                  rk=rwkv_rk[l].reshape(-1), ln_g=rwkv_ln_g[l], ln_b=rwkv_ln_b[l], sb_bias=sb_bias[l],
                  w_up_a=w_up_a[l].astype(BF16), w_up_b=w_up_b[l].astype(BF16),
                  w_out=w_out[l].astype(BF16))
        shift0 = jnp.zeros((bp, rw_proj), F32)
        wkv0 = jnp.zeros((bp, rw_heads, RW_HEAD_DIM, RW_HEAD_DIM), F32)
        res_p = _mixer(xp, shift0, wkv0, None, lw)
        res_s = _mixer(xs, state_shift[l], state_wkv[l], (cache_k, cache_v, l, page_table), lw)
        for i in range(4):
            outs[i].append(res_p[1 + i])
            outs[4 + i].append(res_s[1 + i])
        x1 = jnp.concatenate([res_p[0], res_s[0]], axis=0)
        last = l == depth - 1
        x2 = _moe_layer(x1, norm2_g[l], router_group_w[l], router_group_b[l], router_expert_w[l],
                        router_expert_b[l], expert_w_gate[l], expert_w_up[l], expert_w_down[l],
                        normf_g, last)
        xp = x2[:bp * tp].reshape(bp, tp, d)
        xs = x2[bp * tp:].reshape(bs, ts, d)
    return (xp, xs) + tuple(jnp.stack(o) for o in outs)
```

```python
import functools

import jax
import jax.numpy as jnp
from jax import lax
from jax.experimental import pallas as pl
from jax.experimental.pallas import tpu as pltpu

F32 = jnp.float32
BF16 = jnp.bfloat16

LANES = 128
RW_HEAD_DIM = 64
RW_CHUNK = 64
RW_INV_BLOCK = 16
RW_PAIRS_PER_STEP = 8
RW_GN_EPS = 64e-5
RMS_EPS = 1e-6
TOP_K = 2
MOE_SUB = 128
MOE_UNIT = 3 * MOE_SUB
MOE_DOWN_COLS = 512
VMEM_LIMIT_MB = 56


def _params(sem, vmem_mb=VMEM_LIMIT_MB):
    return pltpu.CompilerParams(dimension_semantics=sem, vmem_limit_bytes=vmem_mb << 20)


def _tile(n, pref, mult=8):
    t = min(pref, n)
    while t >= mult:
        if n % t == 0 and t % mult == 0:
            return t
        t -= 1
    return n


def _dg(a, b, nt=False):
    if a.ndim == 3:
        dims = (((2,), (2 if nt else 1,)), ((0,), (0,)))
    else:
        dims = (((1,), (1 if nt else 0,)), ((), ()))
    return lax.dot_general(a, b, dims, preferred_element_type=F32)


def _split(x):
    hi = x.astype(BF16)
    lo = (x - hi.astype(F32)).astype(BF16)
    return hi, lo


def _dot1(a, b, nt=False):
    return _dg(a.astype(BF16), b.astype(BF16), nt)


def _dot3(a, b, nt=False):
    ah, al = _split(a)
    bh, bl = _split(b)
    return _dg(ah, bh, nt) + (_dg(ah, bl, nt) + _dg(al, bh, nt))


def _dot2x(a, b_exact):
    ah, al = _split(a)
    return _dg(ah, b_exact) + _dg(al, b_exact)


def _sigmoid(x):
    return 1.0 / (1.0 + jnp.exp(-x))


def _softplus(x):
    return jnp.maximum(x, 0.0) + jnp.log1p(jnp.exp(-jnp.abs(x)))


def _softplus_pair(z):
    l = jnp.log(1.0 + jnp.exp(-jnp.abs(z)))
    return jnp.maximum(z, 0.0) + l, jnp.minimum(z, 0.0) - l


def _rmsnorm_kernel(x_ref, g_ref, o_ref):
    x = x_ref[...]
    ms = jnp.mean(x * x, axis=-1, keepdims=True)
    o_ref[...] = (x * lax.rsqrt(ms + RMS_EPS) * g_ref[...]).astype(o_ref.dtype)


def _rmsnorm(x, g, out_dtype):
    m, d = x.shape
    tm = _tile(m, 256)
    return pl.pallas_call(
        _rmsnorm_kernel,
        grid=(m // tm,),
        in_specs=[pl.BlockSpec((tm, d), lambda i: (i, 0)),
                  pl.BlockSpec((1, d), lambda i: (0, 0))],
        out_specs=pl.BlockSpec((tm, d), lambda i: (i, 0)),
        out_shape=jax.ShapeDtypeStruct((m, d), out_dtype),
        compiler_params=_params(("parallel",)),
        name="rmsnorm",
    )(x, g.reshape(1, d))


def _mm_kernel(a_ref, b_ref, o_ref):
    o_ref[...] = jnp.dot(a_ref[...], b_ref[...], preferred_element_type=F32).astype(o_ref.dtype)


def _matmul(a, b, out_dtype, tm_pref, tn_pref):
    m, k = a.shape
    n = b.shape[1]
    tm = _tile(m, tm_pref, 16)
    tn = _tile(n, tn_pref, LANES)
    return pl.pallas_call(
        _mm_kernel,
        grid=(m // tm, n // tn),
        in_specs=[pl.BlockSpec((tm, k), lambda i, j: (i, 0)),
                  pl.BlockSpec((k, tn), lambda i, j: (0, j))],
        out_specs=pl.BlockSpec((tm, tn), lambda i, j: (i, j)),
        out_shape=jax.ShapeDtypeStruct((m, n), out_dtype),
        compiler_params=_params(("parallel", "arbitrary")),
        name="in_proj",
    )(a, b)


def _merge_kernel(oa_ref, ob_ref, wa_ref, wb_ref, ga_ref, gb_ref, o_ref):
    ya = jnp.dot(oa_ref[...], wa_ref[...], preferred_element_type=F32)
    yb = jnp.dot(ob_ref[...], wb_ref[...], preferred_element_type=F32)
    o_ref[...] = (_sigmoid(ga_ref[...]) * ya + _sigmoid(gb_ref[...]) * yb).astype(o_ref.dtype)


def _merge_up(o_a, o_b, proj, w_up_a, w_up_b, ga_off, gb_off):
    m, ka = o_a.shape
    kb = o_b.shape[1]
    d = w_up_a.shape[1]
    tm = _tile(m, 1024, 16)
    tn = 256
    assert d % tn == 0 and ga_off % tn == 0 and gb_off % tn == 0
    ga_blk, gb_blk = ga_off // tn, gb_off // tn
    return pl.pallas_call(
        _merge_kernel,
        grid=(m // tm, d // tn),
        in_specs=[pl.BlockSpec((tm, ka), lambda i, j: (i, 0)),
                  pl.BlockSpec((tm, kb), lambda i, j: (i, 0)),
                  pl.BlockSpec((ka, tn), lambda i, j: (0, j)),
                  pl.BlockSpec((kb, tn), lambda i, j: (0, j)),
                  pl.BlockSpec((tm, tn), lambda i, j: (i, ga_blk + j)),
                  pl.BlockSpec((tm, tn), lambda i, j: (i, gb_blk + j))],
        out_specs=pl.BlockSpec((tm, tn), lambda i, j: (i, j)),
        out_shape=jax.ShapeDtypeStruct((m, d), BF16),
        compiler_params=_params(("parallel", "arbitrary")),
        name="merge_up",
    )(o_a, o_b, w_up_a, w_up_b, proj, proj)


def _outproj_kernel(a_ref, b_ref, x_ref, o_ref):
    o_ref[...] = x_ref[...] + jnp.dot(a_ref[...], b_ref[...], preferred_element_type=F32)


def _out_proj(mixed, w_out, x):
    m, k = mixed.shape
    n = w_out.shape[1]
    tm = _tile(m, 1024, 16)
    tn = _tile(n, 512, LANES)
    return pl.pallas_call(
        _outproj_kernel,
        grid=(m // tm, n // tn),
        in_specs=[pl.BlockSpec((tm, k), lambda i, j: (i, 0)),
                  pl.BlockSpec((k, tn), lambda i, j: (0, j)),
                  pl.BlockSpec((tm, tn), lambda i, j: (i, j))],
        out_specs=pl.BlockSpec((tm, tn), lambda i, j: (i, j)),
        out_shape=jax.ShapeDtypeStruct((m, n), F32),
        compiler_params=_params(("parallel", "arbitrary")),
        name="out_proj",
    )(mixed, w_out, x)


def _rwkv_prep_kernel(p_ref, sh_ref, mu_ref, w0_ref, w2_ref, a0_ref, a2_ref, kkw_ref, kaw_ref,
                      rkw_ref, ones_ref,
                      r_out, lw_out, k_out, v_out, na_out, bb_out, bon_out,
                      carry, *, rww, dl, al):
    t = pl.program_id(1)
    tt = p_ref.shape[1]

    @pl.when(t == 0)
    def _():
        carry[0:1, :] = sh_ref[0]

    row0 = lax.broadcasted_iota(jnp.int32, (tt, 1), 0) == 0

    def mixed(lo, hi):
        p = p_ref[0, :, lo:hi]
        prev = jnp.where(row0, carry[0:1, lo:hi], pltpu.roll(p, 1, 0))
        return p + (prev - p) * mu_ref[:, lo:hi]

    wd = mixed(3 * rww, 3 * rww + dl)
    ad = mixed(3 * rww + dl, 3 * rww + dl + al)
    w_lora = _dot3(jnp.tanh(wd), w2_ref[...])
    a_lora = _dot3(ad, a2_ref[...])
    ones = ones_ref[...]
    for s in range(rww // LANES):
        c0, c1 = s * LANES, (s + 1) * LANES
        r = mixed(c0, c1)
        k = mixed(rww + c0, rww + c1)
        v = mixed(2 * rww + c0, 2 * rww + c1)
        w = w0_ref[:, c0:c1] + w_lora[:, c0:c1]
        lw = -jnp.exp(-_softplus(-w) - 0.5)
        a = _sigmoid(a0_ref[:, c0:c1] + a_lora[:, c0:c1])
        kk = k * kkw_ref[:, c0:c1]
        ss = _dot2x(kk * kk, ones)
        kk = kk / jnp.maximum(jnp.sqrt(ss), 1e-12)
        k2 = k * (1.0 + (a - 1.0) * kaw_ref[:, c0:c1])
        rk = _dot2x(r * k2 * rkw_ref[:, c0:c1], ones)
        r_out[0, :, c0:c1] = r
        lw_out[0, :, c0:c1] = lw
        k_out[0, :, c0:c1] = k2
        v_out[0, :, c0:c1] = v
        na_out[0, :, c0:c1] = -kk
        bb_out[0, :, c0:c1] = kk * a
        bon_out[0, :, c0:c1] = rk * v
    carry[0:1, :] = p_ref[0, tt - 1:tt, :]


def _head_ones():
    i = jnp.arange(LANES) // RW_HEAD_DIM
    return (i[:, None] == i[None, :]).astype(BF16)


def _rwkv_prep(proj, shift0, mu, w0, w2, a0, a2, kkw, kaw, rkw):
    b, t, _ = proj.shape
    p = mu.shape[0]
    rww = w0.shape[0]
    dl, al = w2.shape[0], a2.shape[0]
    tt = _tile(t, 128)
    row = lambda x: x.reshape(1, -1)
    out = jax.ShapeDtypeStruct((b, t, rww), F32)
    vec = lambda n: pl.BlockSpec((1, n), lambda i, j: (0, 0))
    full = lambda s: pl.BlockSpec(s, lambda i, j: (0, 0))
    blk = pl.BlockSpec((1, tt, rww), lambda i, j: (i, j, 0))
    return pl.pallas_call(
        functools.partial(_rwkv_prep_kernel, rww=rww, dl=dl, al=al),
        grid=(b, t // tt),
        in_specs=[pl.BlockSpec((1, tt, p), lambda i, j: (i, j, 0)),
                  pl.BlockSpec((1, 1, p), lambda i, j: (i, 0, 0)),
                  vec(p), vec(rww), full((dl, rww)), vec(rww), full((al, rww)),
                  vec(rww), vec(rww), vec(rww), full((LANES, LANES))],
        out_specs=[blk] * 7,
        out_shape=[out] * 7,
        scratch_shapes=[pltpu.VMEM((8, p), F32)],
        compiler_params=_params(("parallel", "arbitrary")),
        name="rwkv_prep",
    )(proj, shift0.reshape(b, 1, p), row(mu), row(w0), w2, row(a0), a2, row(kkw), row(kaw),
      row(rkw), _head_ones())


def _tri_inv(a, n, nb):
    row = lax.broadcasted_iota(jnp.int32, (n, n), 0)
    col = lax.broadcasted_iota(jnp.int32, (n, n), 1)
    dmask = (row // RW_INV_BLOCK) == (col // RW_INV_BLOCK)
    eye = jnp.where(row == col, 1.0, 0.0)
    ad = jnp.where(dmask, a, 0.0)
    ao = jnp.where(dmask, 0.0, a)
    dinv = eye + ad
    p = ad
    k = 1
    while 2 * k < RW_INV_BLOCK:
        p = _dot1(p, p)
        dinv = dinv + _dot1(dinv, p)
        k *= 2
    bm = _dot1(dinv, ao)
    x = eye + bm
    p = bm
    k = 1
    while 2 * k < nb:
        p = _dot1(p, p)
        x = x + _dot1(x, p)
        k *= 2
    return _dot1(x, dinv)


def _rwkv_chunk_math(r, cum, lw, kk, v, na, bb, s):
    ch = r.shape[1]
    n = 2 * ch
    cl = cum[:, ch - 1:ch, :]
    e_neg = jnp.exp(-cum)
    e_end = jnp.exp(cl - cum)
    a_t = na * jnp.exp(cum - lw)
    r_t = r * jnp.exp(cum)
    lane_lo = lax.broadcasted_iota(jnp.int32, (ch, LANES), 1) < RW_HEAD_DIM

    def hat(x):
        return jnp.concatenate([jnp.where(lane_lo, x, 0.0), jnp.where(lane_lo, 0.0, x)], axis=1)

    ath, rth, vh = hat(a_t), hat(r_t), hat(v)
    btkt = jnp.concatenate([hat(bb * e_neg), hat(kk * e_neg)], axis=1)
    bbh, kbh = hat(bb * e_end), hat(kk * e_end)

    row = lax.broadcasted_iota(jnp.int32, (n, n), 0)
    col = lax.broadcasted_iota(jnp.int32, (n, n), 1)
    strict = row > col
    incl = row >= col
    sc_a = _dot3(ath, btkt, nt=True)
    sc_r = _dot1(rth, btkt, nt=True)
    a_ab = jnp.where(strict, sc_a[:, :, :n], 0.0)
    a_ak = jnp.where(strict, sc_a[:, :, n:], 0.0)
    a_rb = jnp.where(incl, sc_r[:, :, :n], 0.0)
    a_rk = jnp.where(incl, sc_r[:, :, n:], 0.0)

    tinv = _tri_inv(a_ab, n, ch // RW_INV_BLOCK)
    x = _dot1(tinv, jnp.concatenate([_dot1(a_ak, vh), ath], axis=2))
    u0, a2 = x[:, :, :LANES], x[:, :, LANES:]
    z = _dot1(a_rb, x)
    y0 = z[:, :, :LANES] + _dot1(a_rk, vh)
    rh = rth + z[:, :, LANES:]
    yh = _dot1(rh, s, nt=True) + y0
    y = yh[:, :ch] + yh[:, ch:]

    tr = lambda t: jnp.swapaxes(t, 1, 2)
    m = jnp.where(row == col, jnp.exp(cl), 0.0) + _dot1(tr(a2), bbh)
    nn = _dot1(jnp.concatenate([tr(u0), tr(vh)], axis=2), jnp.concatenate([bbh, kbh], axis=1))
    return y, _dot3(s, m) + nn


def _rwkv_chunk_kernel(r_ref, lw_ref, k_ref, v_ref, na_ref, bb_ref, bon_ref, lng_ref, lnb_ref,
                       s0_ref, ltri_ref, ones_ref, o_ref, sf_ref, s_scr):
    c = pl.program_id(2)
    pairs = s_scr.shape[0]
    ch = r_ref.shape[1]

    @pl.when(c == 0)
    def _():
        s_scr[...] = s0_ref[0]

    def pairwise(x):
        return jnp.stack([x[:, g * LANES:(g + 1) * LANES] for g in range(pairs)], axis=0)

    lw = lw_ref[0]
    cum = _dot2x_lhs(ltri_ref[...], lw)
    y, s_new = _rwkv_chunk_math(pairwise(r_ref[0]), pairwise(cum), pairwise(lw), pairwise(k_ref[0]),
                                pairwise(v_ref[0]), pairwise(na_ref[0]), pairwise(bb_ref[0]), s_scr[...])
    s_scr[...] = s_new
    sf_ref[0] = s_new

    ones = ones_ref[...]
    y2 = y.reshape(pairs * ch, LANES)
    mean = _dot2x(y2, ones) * (1.0 / RW_HEAD_DIM)
    d = y2 - mean
    var = _dot2x(d * d, ones) * (1.0 / RW_HEAD_DIM)
    yn = d * lax.rsqrt(var + RW_GN_EPS)
    for g in range(pairs):
        sl = slice(g * LANES, (g + 1) * LANES)
        out = yn[g * ch:(g + 1) * ch] * lng_ref[:, sl] + lnb_ref[:, sl] + bon_ref[0, :, sl]
        o_ref[0, :, sl] = out.astype(o_ref.dtype)


def _dot2x_lhs(a_exact, b):
    bh, bl = _split(b)
    return _dg(a_exact, bh) + _dg(a_exact, bl)


def _rwkv_recurrence(r, lw, k, v, na, bb, bonus, wkv0, ln_g, ln_b):
    b, t, rww = r.shape
    hp = rww // LANES
    hd = RW_HEAD_DIM
    ch = RW_CHUNK
    tp = -(-t // ch) * ch
    if tp != t:
        pad = lambda x: jnp.pad(x, ((0, 0), (0, tp - t), (0, 0)))
        r, lw, k, v, na, bb, bonus = map(pad, (r, lw, k, v, na, bb, bonus))
    nc = tp // ch
    w0 = wkv0.astype(F32).reshape(b, hp, 2, hd, hd)
    zero = jnp.zeros((b, hp, hd, hd), F32)
    s0 = jnp.concatenate([jnp.concatenate([w0[:, :, 0], zero], axis=-1),
                          jnp.concatenate([zero, w0[:, :, 1]], axis=-1)], axis=-2)
    ltri = (jnp.arange(ch)[:, None] >= jnp.arange(ch)[None, :]).astype(BF16)
    pairs = _tile(hp, RW_PAIRS_PER_STEP, 1)
    blk = pl.BlockSpec((1, ch, pairs * LANES), lambda i, j, c: (i, c, j))
    vec = pl.BlockSpec((1, pairs * LANES), lambda i, j, c: (0, j))
    st = pl.BlockSpec((1, pairs, LANES, LANES), lambda i, j, c: (i, j, 0, 0))
    full = lambda s: pl.BlockSpec(s, lambda i, j, c: (0, 0))
    o, sf = pl.pallas_call(
        _rwkv_chunk_kernel,
        grid=(b, hp // pairs, nc),
        in_specs=[blk] * 7 + [vec, vec, st, full((ch, ch)), full((LANES, LANES))],
        out_specs=[blk, st],
        out_shape=[jax.ShapeDtypeStruct((b, tp, rww), BF16),
                   jax.ShapeDtypeStruct((b, hp, LANES, LANES), F32)],
        scratch_shapes=[pltpu.VMEM((pairs, LANES, LANES), F32)],
        compiler_params=_params(("parallel", "parallel", "arbitrary")),
        name="rwkv_chunk",
    )(r, lw, k, v, na, bb, bonus, ln_g.reshape(1, rww), ln_b.reshape(1, rww), s0, ltri, _head_ones())
    sf = sf.reshape(b, hp, 2, hd, 2, hd)
    wkv = jnp.stack([sf[:, :, 0, :, 0, :], sf[:, :, 1, :, 1, :]], axis=2).reshape(b, 2 * hp, hd, hd)
    return o[:, :t], wkv


def _suffix_matrix(tk):
    i = jnp.arange(tk)
    upper = (i[:, None] > i[None, :]).astype(BF16)
    return jnp.concatenate([upper, jnp.ones((tk, tk), BF16)], axis=1)


def _sb_prompt_kernel(bias_ref, q_ref, k_ref, v_ref, mxo_ref, o_ref, acc, carry, *, scale, hd, nh):
    qi = pl.program_id(2)
    tq = q_ref.shape[1]
    mxo = mxo_ref[...]

    def heads_of(ref, rows):
        return jnp.stack([ref[0, rows, i * hd:(i + 1) * hd] for i in range(nh)], axis=0)

    q = (heads_of(q_ref, slice(None)) * scale).astype(BF16)
    m = nh * tq

    def scores(rows):
        z = _dg(q, heads_of(k_ref, rows).astype(BF16), nt=True).reshape(m, rows.size)
        bias = jnp.concatenate([jnp.broadcast_to(bias_ref[i, 0:1, :rows.size], (tq, rows.size))
                                for i in range(nh)], axis=0)
        return _softplus_pair(z + bias)

    def weighted_values(att, rows):
        pv = _dg(att.astype(BF16).reshape(nh, tq, rows.size), heads_of(v_ref, rows).astype(BF16))
        return pv.reshape(m, hd)

    def rows_at(kb, n):
        return pl.ds(pl.multiple_of(kb * tq, tq), n)

    row = lax.broadcasted_iota(jnp.int32, (m, tq), 0) % tq
    col = lax.broadcasted_iota(jnp.int32, (m, tq), 1)
    mask = col < row
    rows = rows_at(qi, tq)
    sp, la = scores(rows)
    lsm = jnp.where(mask, -sp, 0.0)
    suf = _dg(lsm.astype(BF16), mxo)
    att = jnp.where(mask, jnp.exp(la + suf), 0.0)
    acc[...] = weighted_values(att, rows)
    carry[...] = jnp.broadcast_to(suf[:, 0:1] + lsm[:, 0:1], (m, hd))

    def wide(kb, c):
        rows = rows_at(kb - 1, 2 * tq)
        sp, la = scores(rows)
        lsm = -sp
        lsb = lsm.astype(BF16)
        suf = _dg(jnp.concatenate([lsb[:, :tq], lsb[:, tq:]], axis=0), mxo)
        suf_old, suf_new = suf[:m], suf[m:]
        tot_new = suf_new[:, 0:1] + lsm[:, tq:tq + 1]
        tot_old = suf_old[:, 0:1] + lsm[:, 0:1]
        att_old = jnp.exp(la[:, :tq] + suf_old + tot_new)
        att_new = jnp.exp(la[:, tq:] + suf_new)
        pv = weighted_values(jnp.concatenate([att_old, att_new], axis=1), rows)
        cr = carry[...]
        acc[...] += jnp.exp(cr) * pv
        carry[...] = cr + jnp.broadcast_to(tot_old + tot_new, (m, hd))
        return c

    lax.fori_loop(0, qi // 2, lambda i, c: wide(qi - 1 - 2 * i, c), 0)

    @pl.when(qi % 2 == 1)
    def _():
        rows = rows_at(0, tq)
        sp, la = scores(rows)
        att = jnp.exp(la + _dg((-sp).astype(BF16), mxo))
        acc[...] += jnp.exp(carry[...]) * weighted_values(att, rows)

    for i in range(nh):
        o_ref[0, :, i * hd:(i + 1) * hd] = acc[i * tq:(i + 1) * tq, :].astype(o_ref.dtype)


def _sb_prompt(proj, bias, q_off, k_off, v_off, heads, hd):
    b, t, _ = proj.shape
    tq = _tile(t, 256, LANES)
    nh = _tile(heads, SB_HEADS_PER_STEP, 1)
    w = nh * hd
    assert q_off % w == 0 and k_off % w == 0 and v_off % w == 0
    qb, kb, vb = q_off // w, k_off // w, v_off // w
    bias_rows = jnp.broadcast_to(bias.astype(F32)[:, None, None], (heads, 8, 2 * tq))
    return pl.pallas_call(
        functools.partial(_sb_prompt_kernel, scale=hd ** -0.5, hd=hd, nh=nh),
        grid=(b, heads // nh, t // tq),
        in_specs=[pl.BlockSpec((nh, 8, 2 * tq), lambda i, h, j: (h, 0, 0)),
                  pl.BlockSpec((1, tq, w), lambda i, h, j: (i, j, qb + h)),
                  pl.BlockSpec((1, t, w), lambda i, h, j: (i, 0, kb + h)),
                  pl.BlockSpec((1, t, w), lambda i, h, j: (i, 0, vb + h)),
                  pl.BlockSpec((tq, tq), lambda i, h, j: (0, 0))],
        out_specs=pl.BlockSpec((1, tq, w), lambda i, h, j: (i, j, h)),
        out_shape=jax.ShapeDtypeStruct((b, t, heads * hd), BF16),
        scratch_shapes=[pltpu.VMEM((nh * tq, hd), F32), pltpu.VMEM((nh * tq, hd), F32)],
        compiler_params=_params(("parallel", "parallel", "arbitrary")),
        name="sb_prompt",
    )(bias_rows, proj, proj, proj, _suffix_matrix(tq)[:, :tq])


SB_HEADS_PER_STEP = 2
SB_QPAD = 16
SB_PAGES_PER_STEP = 4


def _sb_sample_kernel(pt_ref, q_ref, kn_ref, vn_ref, *refs, heads, hd, tq, scale, pps):
    kp_refs, vp_refs = refs[:pps], refs[pps:2 * pps]
    bias_ref, mxo_ref, o_ref, zbuf, abuf, pvbuf, acc, carry = refs[2 * pps:]
    j = pl.program_id(1)
    nj = pl.num_programs(1)
    qp = SB_QPAD
    rows = heads * qp

    def pages(k_refs, v_refs, masked):
        n = len(k_refs)
        tk = k_refs[0].shape[1] // heads

        def head_rows(ref, h):
            return ref[0, pl.ds(h, tk, stride=heads), :].astype(BF16)

        qs = []
        for h in range(heads):
            qh = jnp.concatenate([q_ref[0, :, h * hd:(h + 1) * hd] * scale,
                                  jnp.zeros((qp - tq, hd), F32)], axis=0)
            qs.append(qh.astype(BF16))
        for s in range(n):
            for h in range(heads):
                zbuf[s, h * qp:(h + 1) * qp, :] = _dg(qs[h], head_rows(k_refs[s], h), nt=True)
        sp, la = _softplus_pair(zbuf[0:n] + bias_ref[...])
        if masked:
            qidx = lax.broadcasted_iota(jnp.int32, (rows, tk), 0) % qp
            kidx = lax.broadcasted_iota(jnp.int32, (rows, tk), 1)
            mask = kidx < qidx
            lsm = jnp.where(mask, -sp, 0.0)
        else:
            lsm = -sp
        st = _dg(lsm.astype(BF16).reshape(n * rows, tk), mxo_ref[...]).reshape(n, rows, 2 * tk)
        att = jnp.exp(la + st[:, :, :tk])
        if masked:
            att = jnp.where(mask, att, 0.0)
        abuf[0:n] = att
        for s in range(n):
            for h in range(heads):
                ah = abuf[s, h * qp:(h + 1) * qp, :].astype(BF16)
                pvbuf[s, h * qp:(h + 1) * qp, :] = _dg(ah, head_rows(v_refs[s], h))
        return st[:, :, tk:tk + hd]

    @pl.when(j == 0)
    def _():
        carry[...] = pages([kn_ref], [vn_ref], True)[0]
        acc[...] = pvbuf[0]

    @pl.when(j > 0)
    def _():
        tots = pages(kp_refs, vp_refs, False)
        cr = carry[...]
        out = acc[...]
        for s in range(pps):
            out = out + jnp.exp(cr) * pvbuf[s]
            cr = cr + tots[s]
        acc[...] = out
        carry[...] = cr

    @pl.when(j == nj - 1)
    def _():
        for h in range(heads):
            o_ref[0, :, h * hd:(h + 1) * hd] = acc[h * qp:h * qp + tq, :]


def _sb_sample(q, k_new, v_new, cache_k, cache_v, layer, page_table, bias):
    nb, tq, sbw = q.shape
    depth, nphys, page, heads, hd = cache_k.shape
    n_pages = page_table.shape[1]
    assert tq <= SB_QPAD and tq <= page and page % LANES == 0
    pad = lambda x: jnp.pad(x.reshape(nb, tq * heads, hd), ((0, 0), (0, (page - tq) * heads), (0, 0)))
    ck = cache_k.reshape(depth * nphys, page * heads, hd)
    cv = cache_v.reshape(depth * nphys, page * heads, hd)
    page_table = page_table + layer * nphys
    rows = heads * SB_QPAD
    bias_rows = jnp.broadcast_to(jnp.repeat(bias.astype(F32), SB_QPAD)[:, None], (rows, page))
    pps = _tile(n_pages, SB_PAGES_PER_STEP, 1)

    def page_map(s):
        def index(i, j, pt):
            return (pt[i * n_pages + n_pages - 1 - (jnp.maximum(j, 1) - 1) * pps - s], 0, 0)
        return index

    seq = lambda i, j, pt: (i, 0, 0)
    const = lambda i, j, pt: (0, 0)
    pages = [pl.BlockSpec((1, page * heads, hd), page_map(s)) for s in range(pps)]
    return pl.pallas_call(
        functools.partial(_sb_sample_kernel, heads=heads, hd=hd, tq=tq, scale=hd ** -0.5, pps=pps),
        grid_spec=pltpu.PrefetchScalarGridSpec(
            num_scalar_prefetch=1,
            grid=(nb, n_pages // pps + 1),
            in_specs=[pl.BlockSpec((1, tq, sbw), seq),
                      pl.BlockSpec((1, page * heads, hd), seq),
                      pl.BlockSpec((1, page * heads, hd), seq)] + pages + pages +
                     [pl.BlockSpec((rows, page), const),
                      pl.BlockSpec((page, 2 * page), const)],
            out_specs=pl.BlockSpec((1, tq, sbw), seq),
            scratch_shapes=[pltpu.VMEM((pps, rows, page), F32), pltpu.VMEM((pps, rows, page), F32),
                            pltpu.VMEM((pps, rows, hd), F32),
                            pltpu.VMEM((rows, hd), F32), pltpu.VMEM((rows, hd), F32)]),
        out_shape=jax.ShapeDtypeStruct((nb, tq, sbw), F32),
        compiler_params=_params(("parallel", "arbitrary")),
        name="sb_sample",
    )(page_table.reshape(-1).astype(jnp.int32), q, pad(k_new), pad(v_new), *([ck] * pps), *([cv] * pps),
      bias_rows, _suffix_matrix(page))


def _router_kernel(x_ref, g_ref, rw_ref, rb_ref, h_ref, id_ref, wt_ref, *, n_groups, per_group):
    x = x_ref[...]
    ms = jnp.mean(x * x, axis=-1, keepdims=True)
    h = x * lax.rsqrt(ms + RMS_EPS) * g_ref[...]
    h_ref[...] = h
    logits = _dot3(h, rw_ref[...]) + rb_ref[...]
    tm = logits.shape[0]
    lane = lax.broadcasted_iota(jnp.int32, (tm, LANES), 1)
    lane_f = lane.astype(F32)
    neg = jnp.finfo(F32).min
    far = float(LANES)
    is_g = lane < n_groups
    gl = jnp.where(is_g, logits, neg)
    gmax = jnp.max(gl, axis=1, keepdims=True)
    gidx = jnp.min(jnp.where(gl == gmax, lane_f, far), axis=1, keepdims=True).astype(jnp.int32)
    gprob = 1.0 / jnp.sum(jnp.where(is_g, jnp.exp(logits - gmax), 0.0), axis=1, keepdims=True)
    lo = n_groups + gidx * per_group
    el = jnp.where((lane >= lo) & (lane < lo + per_group), logits, neg)
    v1 = jnp.max(el, axis=1, keepdims=True)
    i1 = jnp.min(jnp.where(el == v1, lane_f, far), axis=1, keepdims=True).astype(jnp.int32)
    el2 = jnp.where(lane == i1, neg, el)
    v2 = jnp.max(el2, axis=1, keepdims=True)
    i2 = jnp.min(jnp.where((el2 == v2) & (lane != i1), lane_f, far), axis=1, keepdims=True).astype(jnp.int32)
    t = jnp.exp(v2 - v1)
    w1 = gprob / (1.0 + t)
    w2 = gprob * t / (1.0 + t)
    id_ref[...] = jnp.where(lane == 0, i1 - n_groups, jnp.where(lane == 1, i2 - n_groups, 0))
    wt_ref[...] = jnp.where(lane == 0, w1, jnp.where(lane == 1, w2, 0.0))


def _router(x, g, rg_w, rg_b, re_w, re_b):
    m, d = x.shape
    n_groups = rg_w.shape[1]
    n_exp = re_w.shape[1]
    assert n_groups + n_exp <= LANES
    padc = LANES - n_groups - n_exp
    rw = jnp.concatenate([rg_w, re_w, jnp.zeros((d, padc), F32)], axis=1).astype(F32)
    rb = jnp.concatenate([rg_b, re_b, jnp.zeros((padc,), F32)]).reshape(1, LANES).astype(F32)
    tm = _tile(m, 256)
    row = pl.BlockSpec((tm, d), lambda i: (i, 0))
    nar = pl.BlockSpec((tm, LANES), lambda i: (i, 0))
    h, ids, wts = pl.pallas_call(
        functools.partial(_router_kernel, n_groups=n_groups, per_group=n_exp // n_groups),
        grid=(m // tm,),
        in_specs=[row, pl.BlockSpec((1, d), lambda i: (0, 0)),
                  pl.BlockSpec((d, LANES), lambda i: (0, 0)),
                  pl.BlockSpec((1, LANES), lambda i: (0, 0))],
        out_specs=[row, nar, nar],
        out_shape=[jax.ShapeDtypeStruct((m, d), F32),
                   jax.ShapeDtypeStruct((m, LANES), jnp.int32),
                   jax.ShapeDtypeStruct((m, LANES), F32)],
        compiler_params=_params(("parallel",)),
        name="router",
    )(x, g.reshape(1, d), rw, rb)
    return h, ids[:, :TOP_K], wts


def _moe_kernel(ue_ref, us_ref, un_ref, tok_ref, h_hbm, wg_ref, wu_ref, wd_ref, y_ref,
                xf, xb, wgb, wub, wdb, sem):
    u = pl.program_id(0)
    f = pl.program_id(1)
    n = un_ref[u]

    def row_copy(tok, i):
        return pltpu.make_async_copy(h_hbm.at[pl.ds(tok, 1), :], xf.at[pl.ds(i, 1), :], sem)

    @pl.when((f == 0) & (n > 0))
    def _():
        xf[...] = jnp.zeros_like(xf)
        base = us_ref[u]

        def start(i, c):
            row_copy(tok_ref[base + i], i).start()
            return c

        def wait(i, c):
            row_copy(0, i).wait()
            return c

        lax.fori_loop(0, n, start, 0)
        lax.fori_loop(0, n, wait, 0)
        xb[...] = xf[...].astype(BF16)

    @pl.when((f == 0) & (n == 0))
    def _():
        y_ref[...] = jnp.zeros_like(y_ref)

    @pl.when(n > 0)
    def _():
        wgb[...] = wg_ref[0].astype(BF16)
        wub[...] = wu_ref[0].astype(BF16)
        wdb[...] = wd_ref[0].astype(BF16)
        n_sub = (n + (MOE_SUB - 1)) // MOE_SUB
        for k in range(1, MOE_UNIT // MOE_SUB + 1):
            m_rows = k * MOE_SUB

            @pl.when(n_sub == k)
            def _():
                x = xb[0:m_rows, :]
                g = _dg(x, wgb[...])
                up = _dg(x, wub[...])
                hid = (g * _sigmoid(g) * up).astype(BF16)
                d = y_ref.shape[1]
                tn = _tile(d, MOE_DOWN_COLS, LANES)

                @pl.when(f == 0)
                def _():
                    for c0 in range(0, d, tn):
                        y_ref[0:m_rows, c0:c0 + tn] = _dg(hid, wdb[:, c0:c0 + tn])
                    if m_rows < MOE_UNIT:
                        y_ref[m_rows:MOE_UNIT, :] = jnp.zeros((MOE_UNIT - m_rows, d), F32)

                @pl.when(f > 0)
                def _():
                    for c0 in range(0, d, tn):
                        y_ref[0:m_rows, c0:c0 + tn] += _dg(hid, wdb[:, c0:c0 + tn])


def _moe_experts(h, tok_sorted, unit_e, unit_start, unit_n, w_gate, w_up, w_down):
    m, d = h.shape
    n_units = unit_e.shape[0]
    de = w_gate.shape[2]
    tf = _tile(de, 256, LANES)
    nf = de // tf

    def fidx(u, f, un):
        return jnp.where(un[u] > 0, f, nf - 1)

    return pl.pallas_call(
        _moe_kernel,
        grid_spec=pltpu.PrefetchScalarGridSpec(
            num_scalar_prefetch=4,
            grid=(n_units, nf),
            in_specs=[pl.BlockSpec(memory_space=pl.ANY),
                      pl.BlockSpec((1, d, tf), lambda u, f, ue, us, un, tok: (ue[u], 0, fidx(u, f, un))),
                      pl.BlockSpec((1, d, tf), lambda u, f, ue, us, un, tok: (ue[u], 0, fidx(u, f, un))),
                      pl.BlockSpec((1, tf, d), lambda u, f, ue, us, un, tok: (ue[u], fidx(u, f, un), 0))],
            out_specs=pl.BlockSpec((MOE_UNIT, d), lambda u, f, ue, us, un, tok: (u, 0)),
            scratch_shapes=[pltpu.VMEM((MOE_UNIT, d), F32), pltpu.VMEM((MOE_UNIT, d), BF16),
                            pltpu.VMEM((d, tf), BF16), pltpu.VMEM((d, tf), BF16),
                            pltpu.VMEM((tf, d), BF16), pltpu.SemaphoreType.DMA(())]),
        out_shape=jax.ShapeDtypeStruct((n_units * MOE_UNIT, d), F32),
        compiler_params=_params(("arbitrary", "arbitrary")),
        name="moe_experts",
    )(unit_e, unit_start, unit_n, tok_sorted, h, w_gate, w_up, w_down)


def _combine_kernel(pos_ref, ys_hbm, w_ref, x_ref, g_ref, o_ref, buf, sem, *, normalize):
    i = pl.program_id(0)
    tc = x_ref.shape[0]
    base = i * tc * TOP_K

    def row_copy(p, s, t):
        return pltpu.make_async_copy(ys_hbm.at[pl.ds(p, 1), :], buf.at[s, pl.ds(t, 1), :], sem)

    def start(t, c):
        for s in range(TOP_K):
            row_copy(pos_ref[base + TOP_K * t + s], s, t).start()
        return c

    def wait(t, c):
        for s in range(TOP_K):
            row_copy(0, s, t).wait()
        return c

    lax.fori_loop(0, tc, start, 0)
    lax.fori_loop(0, tc, wait, 0)
    w = w_ref[...]
    x = x_ref[...] + (w[:, 0:1] * buf[0] + w[:, 1:2] * buf[1])
    if normalize:
        ms = jnp.mean(x * x, axis=-1, keepdims=True)
        x = x * lax.rsqrt(ms + RMS_EPS) * g_ref[...]
    o_ref[...] = x


def _moe_combine(ys, pos, wts, x, g, normalize):
    m, d = x.shape
    tc = _tile(m, 128)
    return pl.pallas_call(
        functools.partial(_combine_kernel, normalize=normalize),
        grid_spec=pltpu.PrefetchScalarGridSpec(
            num_scalar_prefetch=1,
            grid=(m // tc,),
            in_specs=[pl.BlockSpec(memory_space=pl.ANY),
                      pl.BlockSpec((tc, LANES), lambda i, p: (i, 0)),
                      pl.BlockSpec((tc, d), lambda i, p: (i, 0)),
                      pl.BlockSpec((1, d), lambda i, p: (0, 0))],
            out_specs=pl.BlockSpec((tc, d), lambda i, p: (i, 0)),
            scratch_shapes=[pltpu.VMEM((TOP_K, tc, d), F32), pltpu.SemaphoreType.DMA(())]),
        out_shape=jax.ShapeDtypeStruct((m, d), F32),
        compiler_params=_params(("arbitrary",)),
        name="moe_combine",
    )(pos, ys, wts, x, g.reshape(1, d))


def _moe_layer(x, norm_g, rg_w, rg_b, re_w, re_b, w_gate, w_up, w_down, final_g, normalize):
    m, d = x.shape
    n_exp = re_w.shape[1]
    h, expert, wts = _router(x, norm_g, rg_w, rg_b, re_w, re_b)

    n_rows = m * TOP_K
    flat_e = expert.reshape(-1)
    order = jnp.argsort(flat_e).astype(jnp.int32)
    e_sorted = flat_e[order]
    tok_sorted = order // TOP_K
    counts = jnp.zeros((n_exp,), jnp.int32).at[flat_e].add(1)
    start = jnp.cumsum(counts) - counts
    units_e = (counts + MOE_UNIT - 1) // MOE_UNIT
    ucum = jnp.cumsum(units_e)
    ufirst = ucum - units_e
    n_units = n_exp + -(-n_rows // MOE_UNIT)
    uid = jnp.arange(n_units, dtype=jnp.int32)
    total = ucum[-1]
    ue = jnp.minimum(jnp.searchsorted(ucum, uid, side="right"), n_exp - 1).astype(jnp.int32)
    local = uid - ufirst[ue]
    active = uid < total
    un = jnp.where(active, jnp.clip(counts[ue] - local * MOE_UNIT, 0, MOE_UNIT), 0).astype(jnp.int32)
    us = jnp.where(active, start[ue] + local * MOE_UNIT, 0).astype(jnp.int32)
    last_e = ue[jnp.maximum(total - 1, 0)]
    ue = jnp.where(active, ue, last_e).astype(jnp.int32)
    local_row = jnp.arange(n_rows, dtype=jnp.int32) - start[e_sorted]
    pos_sorted = (ufirst[e_sorted] + local_row // MOE_UNIT) * MOE_UNIT + local_row % MOE_UNIT
    pos = jnp.zeros((n_rows,), jnp.int32).at[order].set(pos_sorted.astype(jnp.int32))

    ys = _moe_experts(h, tok_sorted, ue, us, un, w_gate, w_up, w_down)
    return _moe_combine(ys, pos, wts, x, final_g, normalize)


def _mixer(x, shift0, wkv0, past, lw):
    b, t, d = x.shape
    rw_proj = lw["mu"].shape[0]
    rww = lw["w_up_a"].shape[0]
    sbw = lw["w_up_b"].shape[0]
    heads = lw["sb_bias"].shape[0]
    hd = sbw // heads
    q_off, k_off, v_off = rw_proj, rw_proj + sbw, rw_proj + 2 * sbw
    ga_off, gb_off = rw_proj + 3 * sbw, rw_proj + 3 * sbw + d
    x2 = x.reshape(b * t, d)

    h = _rmsnorm(x2, lw["norm1_g"], BF16)
    proj = _matmul(h, lw["w_in"], F32, 1024, 768)
    proj3 = proj.reshape(b, t, -1)

    r, lgw, k2, v, na, bb, bonus = _rwkv_prep(proj3, shift0, lw["mu"], lw["w0"], lw["w2"], lw["a0"],
                                              lw["a2"], lw["kk"], lw["ka"], lw["rk"])
    o_a, wkv_new = _rwkv_recurrence(r, lgw, k2, v, na, bb, bonus, wkv0, lw["ln_g"], lw["ln_b"])
    shift_new = proj3[:, -1, :rw_proj]

    k_new = proj3[:, :, k_off:k_off + sbw]
    v_new = proj3[:, :, v_off:v_off + sbw]
    if past is None:
        o_b = _sb_prompt(proj3, lw["sb_bias"], q_off, k_off, v_off, heads, hd)
    else:
        cache_k, cache_v, layer, page_table = past
        q = proj3[:, :, q_off:q_off + sbw]
        o_b = _sb_sample(q, k_new, v_new, cache_k, cache_v, layer, page_table, lw["sb_bias"]).astype(BF16)

    mixed = _merge_up(o_a.reshape(b * t, rww), o_b.reshape(b * t, sbw), proj, lw["w_up_a"], lw["w_up_b"],
                      ga_off, gb_off)
    x1 = _out_proj(mixed, lw["w_out"], x2)
    return (x1, k_new.reshape(b, t, heads, hd), v_new.reshape(b, t, heads, hd), wkv_new, shift_new)


def kernel(x_prompt, x_sample, cache_k, cache_v, state_wkv, state_shift, page_table, norm1_g, w_in, rwkv_mu, rwkv_w0, rwkv_w2, rwkv_a0, rwkv_a2, rwkv_kk, rwkv_ka, rwkv_rk, rwkv_ln_g, rwkv_ln_b, sb_bias, w_up_a, w_up_b, w_out, norm2_g, router_group_w, router_group_b, router_expert_w, router_expert_b, expert_w_gate, expert_w_up, expert_w_down, normf_g):
    depth = w_in.shape[0]
    bp, tp, d = x_prompt.shape
    bs, ts, _ = x_sample.shape
    rw_proj = rwkv_mu.shape[1]
    rww = w_up_a.shape[1]
    rw_heads = rww // RW_HEAD_DIM
    xp, xs = x_prompt, x_sample
    outs = [[] for _ in range(8)]
    for l in range(depth):
        lw = dict(norm1_g=norm1_g[l], w_in=w_in[l].astype(BF16), mu=rwkv_mu[l], w0=rwkv_w0[l],
                  w2=rwkv_w2[l], a0=rwkv_a0[l], a2=rwkv_a2[l], kk=rwkv_kk[l], ka=rwkv_ka[l],
                  rk=rwkv_rk[l].reshape(-1), ln_g=rwkv_ln_g[l], ln_b=rwkv_ln_b[l], sb_bias=sb_bias[l],
                  w_up_a=w_up_a[l].astype(BF16), w_up_b=w_up_b[l].astype(BF16),
                  w_out=w_out[l].astype(BF16))
        shift0 = jnp.zeros((bp, rw_proj), F32)
        wkv0 = jnp.zeros((bp, rw_heads, RW_HEAD_DIM, RW_HEAD_DIM), F32)
        res_p = _mixer(xp, shift0, wkv0, None, lw)
        res_s = _mixer(xs, state_shift[l], state_wkv[l], (cache_k, cache_v, l, page_table), lw)
        for i in range(4):
            outs[i].append(res_p[1 + i])
            outs[4 + i].append(res_s[1 + i])
        x1 = jnp.concatenate([res_p[0], res_s[0]], axis=0)
        last = l == depth - 1
        x2 = _moe_layer(x1, norm2_g[l], router_group_w[l], router_group_b[l], router_expert_w[l],
                        router_expert_b[l], expert_w_gate[l], expert_w_up[l], expert_w_down[l],
                        normf_g, last)
        xp = x2[:bp * tp].reshape(bp, tp, d)
        xs = x2[bp * tp:].reshape(bs, ts, d)
    return (xp, xs) + tuple(jnp.stack(o) for o in outs)
```

```python
import functools

import jax
import jax.numpy as jnp
from jax import lax
from jax.experimental import pallas as pl
from jax.experimental.pallas import tpu as pltpu

F32 = jnp.float32
BF16 = jnp.bfloat16

LANES = 128
RW_HEAD_DIM = 64
RW_CHUNK = 64
RW_INV_BLOCK = 16
RW_PAIRS_PER_STEP = 8
RW_GN_EPS = 64e-5
RMS_EPS = 1e-6
TOP_K = 2
MOE_SUB = 128
MOE_UNIT = 3 * MOE_SUB
MOE_DOWN_COLS = 512
VMEM_LIMIT_MB = 56


def _params(sem, vmem_mb=VMEM_LIMIT_MB):
    return pltpu.CompilerParams(dimension_semantics=sem, vmem_limit_bytes=vmem_mb << 20)


def _tile(n, pref, mult=8):
    t = min(pref, n)
    while t >= mult:
        if n % t == 0 and t % mult == 0:
            return t
        t -= 1
    return n


def _dg(a, b, nt=False):
    if a.ndim == 3:
        dims = (((2,), (2 if nt else 1,)), ((0,), (0,)))
    else:
        dims = (((1,), (1 if nt else 0,)), ((), ()))
    return lax.dot_general(a, b, dims, preferred_element_type=F32)


def _split(x):
    hi = x.astype(BF16)
    lo = (x - hi.astype(F32)).astype(BF16)
    return hi, lo


def _dot1(a, b, nt=False):
    return _dg(a.astype(BF16), b.astype(BF16), nt)


def _dot3(a, b, nt=False):
    ah, al = _split(a)
    bh, bl = _split(b)
    return _dg(ah, bh, nt) + (_dg(ah, bl, nt) + _dg(al, bh, nt))


def _dot2x(a, b_exact):
    ah, al = _split(a)
    return _dg(ah, b_exact) + _dg(al, b_exact)


def _sigmoid(x):
    return 1.0 / (1.0 + jnp.exp(-x))


def _softplus(x):
    return jnp.maximum(x, 0.0) + jnp.log1p(jnp.exp(-jnp.abs(x)))


def _softplus_pair(z):
    l = jnp.log(1.0 + jnp.exp(-jnp.abs(z)))
    return jnp.maximum(z, 0.0) + l, jnp.minimum(z, 0.0) - l


def _rmsnorm_kernel(x_ref, g_ref, o_ref):
    x = x_ref[...]
    ms = jnp.mean(x * x, axis=-1, keepdims=True)
    o_ref[...] = (x * lax.rsqrt(ms + RMS_EPS) * g_ref[...]).astype(o_ref.dtype)


def _rmsnorm(x, g, out_dtype):
    m, d = x.shape
    tm = _tile(m, 256)
    return pl.pallas_call(
        _rmsnorm_kernel,
        grid=(m // tm,),
        in_specs=[pl.BlockSpec((tm, d), lambda i: (i, 0)),
                  pl.BlockSpec((1, d), lambda i: (0, 0))],
        out_specs=pl.BlockSpec((tm, d), lambda i: (i, 0)),
        out_shape=jax.ShapeDtypeStruct((m, d), out_dtype),
        compiler_params=_params(("parallel",)),
        name="rmsnorm",
    )(x, g.reshape(1, d))


def _mm_kernel(a_ref, b_ref, o_ref):
    o_ref[...] = jnp.dot(a_ref[...], b_ref[...], preferred_element_type=F32).astype(o_ref.dtype)


def _matmul(a, b, out_dtype, tm_pref, tn_pref):
    m, k = a.shape
    n = b.shape[1]
    tm = _tile(m, tm_pref, 16)
    tn = _tile(n, tn_pref, LANES)
    return pl.pallas_call(
        _mm_kernel,
        grid=(m // tm, n // tn),
        in_specs=[pl.BlockSpec((tm, k), lambda i, j: (i, 0)),
                  pl.BlockSpec((k, tn), lambda i, j: (0, j))],
        out_specs=pl.BlockSpec((tm, tn), lambda i, j: (i, j)),
        out_shape=jax.ShapeDtypeStruct((m, n), out_dtype),
        compiler_params=_params(("parallel", "arbitrary")),
        name="in_proj",
    )(a, b)


def _merge_kernel(oa_ref, ob_ref, wa_ref, wb_ref, ga_ref, gb_ref, o_ref):
    ya = jnp.dot(oa_ref[...], wa_ref[...], preferred_element_type=F32)
    yb = jnp.dot(ob_ref[...], wb_ref[...], preferred_element_type=F32)
    o_ref[...] = (_sigmoid(ga_ref[...]) * ya + _sigmoid(gb_ref[...]) * yb).astype(o_ref.dtype)


def _merge_up(o_a, o_b, proj, w_up_a, w_up_b, ga_off, gb_off):
    m, ka = o_a.shape
    kb = o_b.shape[1]
    d = w_up_a.shape[1]
    tm = _tile(m, 1024, 16)
    tn = 256
    assert d % tn == 0 and ga_off % tn == 0 and gb_off % tn == 0
    ga_blk, gb_blk = ga_off // tn, gb_off // tn
    return pl.pallas_call(
        _merge_kernel,
        grid=(m // tm, d // tn),
        in_specs=[pl.BlockSpec((tm, ka), lambda i, j: (i, 0)),
                  pl.BlockSpec((tm, kb), lambda i, j: (i, 0)),
                  pl.BlockSpec((ka, tn), lambda i, j: (0, j)),
                  pl.BlockSpec((kb, tn), lambda i, j: (0, j)),
                  pl.BlockSpec((tm, tn), lambda i, j: (i, ga_blk + j)),
                  pl.BlockSpec((tm, tn), lambda i, j: (i, gb_blk + j))],
        out_specs=pl.BlockSpec((tm, tn), lambda i, j: (i, j)),
        out_shape=jax.ShapeDtypeStruct((m, d), BF16),
        compiler_params=_params(("parallel", "arbitrary")),
        name="merge_up",
    )(o_a, o_b, w_up_a, w_up_b, proj, proj)


def _outproj_kernel(a_ref, b_ref, x_ref, o_ref):
    o_ref[...] = x_ref[...] + jnp.dot(a_ref[...], b_ref[...], preferred_element_type=F32)


def _out_proj(mixed, w_out, x):
    m, k = mixed.shape
    n = w_out.shape[1]
    tm = _tile(m, 1024, 16)
    tn = _tile(n, 512, LANES)
    return pl.pallas_call(
        _outproj_kernel,
        grid=(m // tm, n // tn),
        in_specs=[pl.BlockSpec((tm, k), lambda i, j: (i, 0)),
                  pl.BlockSpec((k, tn), lambda i, j: (0, j)),
                  pl.BlockSpec((tm, tn), lambda i, j: (i, j))],
        out_specs=pl.BlockSpec((tm, tn), lambda i, j: (i, j)),
        out_shape=jax.ShapeDtypeStruct((m, n), F32),
        compiler_params=_params(("parallel", "arbitrary")),
        name="out_proj",
    )(mixed, w_out, x)


def _rwkv_prep_kernel(p_ref, sh_ref, mu_ref, w0_ref, w2_ref, a0_ref, a2_ref, kkw_ref, kaw_ref,
                      rkw_ref, ones_ref,
                      r_out, lw_out, k_out, v_out, na_out, bb_out, bon_out,
                      carry, *, rww, dl, al):
    t = pl.program_id(1)
    tt = p_ref.shape[1]

    @pl.when(t == 0)
    def _():
        carry[0:1, :] = sh_ref[0]

    row0 = lax.broadcasted_iota(jnp.int32, (tt, 1), 0) == 0

    def mixed(lo, hi):
        p = p_ref[0, :, lo:hi]
        prev = jnp.where(row0, carry[0:1, lo:hi], pltpu.roll(p, 1, 0))
        return p + (prev - p) * mu_ref[:, lo:hi]

    wd = mixed(3 * rww, 3 * rww + dl)
    ad = mixed(3 * rww + dl, 3 * rww + dl + al)
    w_lora = _dot3(jnp.tanh(wd), w2_ref[...])
    a_lora = _dot3(ad, a2_ref[...])
    ones = ones_ref[...]
    for s in range(rww // LANES):
        c0, c1 = s * LANES, (s + 1) * LANES
        r = mixed(c0, c1)
        k = mixed(rww + c0, rww + c1)
        v = mixed(2 * rww + c0, 2 * rww + c1)
        w = w0_ref[:, c0:c1] + w_lora[:, c0:c1]
        lw = -jnp.exp(-_softplus(-w) - 0.5)
        a = _sigmoid(a0_ref[:, c0:c1] + a_lora[:, c0:c1])
        kk = k * kkw_ref[:, c0:c1]
        ss = _dot2x(kk * kk, ones)
        kk = kk / jnp.maximum(jnp.sqrt(ss), 1e-12)
        k2 = k * (1.0 + (a - 1.0) * kaw_ref[:, c0:c1])
        rk = _dot2x(r * k2 * rkw_ref[:, c0:c1], ones)
        r_out[0, :, c0:c1] = r
        lw_out[0, :, c0:c1] = lw
        k_out[0, :, c0:c1] = k2
        v_out[0, :, c0:c1] = v
        na_out[0, :, c0:c1] = -kk
        bb_out[0, :, c0:c1] = kk * a
        bon_out[0, :, c0:c1] = rk * v
    carry[0:1, :] = p_ref[0, tt - 1:tt, :]


def _head_ones():
    i = jnp.arange(LANES) // RW_HEAD_DIM
    return (i[:, None] == i[None, :]).astype(BF16)


def _rwkv_prep(proj, shift0, mu, w0, w2, a0, a2, kkw, kaw, rkw):
    b, t, _ = proj.shape
    p = mu.shape[0]
    rww = w0.shape[0]
    dl, al = w2.shape[0], a2.shape[0]
    tt = _tile(t, 128)
    row = lambda x: x.reshape(1, -1)
    out = jax.ShapeDtypeStruct((b, t, rww), F32)
    vec = lambda n: pl.BlockSpec((1, n), lambda i, j: (0, 0))
    full = lambda s: pl.BlockSpec(s, lambda i, j: (0, 0))
    blk = pl.BlockSpec((1, tt, rww), lambda i, j: (i, j, 0))
    return pl.pallas_call(
        functools.partial(_rwkv_prep_kernel, rww=rww, dl=dl, al=al),
        grid=(b, t // tt),
        in_specs=[pl.BlockSpec((1, tt, p), lambda i, j: (i, j, 0)),
                  pl.BlockSpec((1, 1, p), lambda i, j: (i, 0, 0)),
                  vec(p), vec(rww), full((dl, rww)), vec(rww), full((al, rww)),
                  vec(rww), vec(rww), vec(rww), full((LANES, LANES))],
        out_specs=[blk] * 7,
        out_shape=[out] * 7,
        scratch_shapes=[pltpu.VMEM((8, p), F32)],
        compiler_params=_params(("parallel", "arbitrary")),
        name="rwkv_prep",
    )(proj, shift0.reshape(b, 1, p), row(mu), row(w0), w2, row(a0), a2, row(kkw), row(kaw),
      row(rkw), _head_ones())


def _tri_inv(a, n, nb):
    row = lax.broadcasted_iota(jnp.int32, (n, n), 0)
    col = lax.broadcasted_iota(jnp.int32, (n, n), 1)
    dmask = (row // RW_INV_BLOCK) == (col // RW_INV_BLOCK)
    eye = jnp.where(row == col, 1.0, 0.0)
    ad = jnp.where(dmask, a, 0.0)
    ao = jnp.where(dmask, 0.0, a)
    dinv = eye + ad
    p = ad
    k = 1
    while 2 * k < RW_INV_BLOCK:
        p = _dot1(p, p)
        dinv = dinv + _dot1(dinv, p)
        k *= 2
    bm = _dot1(dinv, ao)
    x = eye + bm
    p = bm
    k = 1
    while 2 * k < nb:
        p = _dot1(p, p)
        x = x + _dot1(x, p)
        k *= 2
    return _dot1(x, dinv)


def _rwkv_chunk_math(r, cum, lw, kk, v, na, bb, s):
    ch = r.shape[1]
    n = 2 * ch
    cl = cum[:, ch - 1:ch, :]
    e_neg = jnp.exp(-cum)
    e_end = jnp.exp(cl - cum)
    a_t = na * jnp.exp(cum - lw)
    r_t = r * jnp.exp(cum)
    lane_lo = lax.broadcasted_iota(jnp.int32, (ch, LANES), 1) < RW_HEAD_DIM

    def hat(x):
        return jnp.concatenate([jnp.where(lane_lo, x, 0.0), jnp.where(lane_lo, 0.0, x)], axis=1)

    ath, rth, vh = hat(a_t), hat(r_t), hat(v)
    btkt = jnp.concatenate([hat(bb * e_neg), hat(kk * e_neg)], axis=1)
    bbh, kbh = hat(bb * e_end), hat(kk * e_end)

    row = lax.broadcasted_iota(jnp.int32, (n, n), 0)
    col = lax.broadcasted_iota(jnp.int32, (n, n), 1)
    strict = row > col
    incl = row >= col
    sc = _dot1(jnp.concatenate([ath, rth], axis=1), btkt, nt=True)
    a_ab = jnp.where(strict, sc[:, :n, :n], 0.0)
    a_ak = jnp.where(strict, sc[:, :n, n:], 0.0)
    a_rb = jnp.where(incl, sc[:, n:, :n], 0.0)
    a_rk = jnp.where(incl, sc[:, n:, n:], 0.0)

    tinv = _tri_inv(a_ab, n, ch // RW_INV_BLOCK)
    x = _dot1(tinv, jnp.concatenate([_dot1(a_ak, vh), ath], axis=2))
    u0, a2 = x[:, :, :LANES], x[:, :, LANES:]
    z = _dot1(a_rb, x)
    y0 = z[:, :, :LANES] + _dot1(a_rk, vh)
    rh = rth + z[:, :, LANES:]
    yh = _dot1(rh, s, nt=True) + y0
    y = yh[:, :ch] + yh[:, ch:]

    tr = lambda t: jnp.swapaxes(t, 1, 2)
    m = jnp.where(row == col, jnp.exp(cl), 0.0) + _dot1(tr(a2), bbh)
    nn = _dot1(jnp.concatenate([tr(u0), tr(vh)], axis=2), jnp.concatenate([bbh, kbh], axis=1))
    return y, _dot3(s, m) + nn


def _rwkv_chunk_kernel(r_ref, lw_ref, k_ref, v_ref, na_ref, bb_ref, bon_ref, lng_ref, lnb_ref,
                       s0_ref, ltri_ref, ones_ref, o_ref, sf_ref, s_scr):
    c = pl.program_id(2)
    pairs = s_scr.shape[0]
    ch = r_ref.shape[1]

    @pl.when(c == 0)
    def _():
        s_scr[...] = s0_ref[0]

    def pairwise(x):
        return jnp.stack([x[:, g * LANES:(g + 1) * LANES] for g in range(pairs)], axis=0)

    lw = lw_ref[0]
    cum = _dot2x_lhs(ltri_ref[...], lw)
    y, s_new = _rwkv_chunk_math(pairwise(r_ref[0]), pairwise(cum), pairwise(lw), pairwise(k_ref[0]),
                                pairwise(v_ref[0]), pairwise(na_ref[0]), pairwise(bb_ref[0]), s_scr[...])
    s_scr[...] = s_new
    sf_ref[0] = s_new

    ones = ones_ref[...]
    y2 = y.reshape(pairs * ch, LANES)
    mean = _dot2x(y2, ones) * (1.0 / RW_HEAD_DIM)
    d = y2 - mean
    var = _dot2x(d * d, ones) * (1.0 / RW_HEAD_DIM)
    yn = d * lax.rsqrt(var + RW_GN_EPS)
    for g in range(pairs):
        sl = slice(g * LANES, (g + 1) * LANES)
        out = yn[g * ch:(g + 1) * ch] * lng_ref[:, sl] + lnb_ref[:, sl] + bon_ref[0, :, sl]
        o_ref[0, :, sl] = out.astype(o_ref.dtype)


def _dot2x_lhs(a_exact, b):
    bh, bl = _split(b)
    return _dg(a_exact, bh) + _dg(a_exact, bl)


def _rwkv_recurrence(r, lw, k, v, na, bb, bonus, wkv0, ln_g, ln_b):
    b, t, rww = r.shape
    hp = rww // LANES
    hd = RW_HEAD_DIM
    ch = RW_CHUNK
    tp = -(-t // ch) * ch
    if tp != t:
        pad = lambda x: jnp.pad(x, ((0, 0), (0, tp - t), (0, 0)))
        r, lw, k, v, na, bb, bonus = map(pad, (r, lw, k, v, na, bb, bonus))
    nc = tp // ch
    w0 = wkv0.astype(F32).reshape(b, hp, 2, hd, hd)
    zero = jnp.zeros((b, hp, hd, hd), F32)
    s0 = jnp.concatenate([jnp.concatenate([w0[:, :, 0], zero], axis=-1),
                          jnp.concatenate([zero, w0[:, :, 1]], axis=-1)], axis=-2)
    ltri = (jnp.arange(ch)[:, None] >= jnp.arange(ch)[None, :]).astype(BF16)
    pairs = _tile(hp, RW_PAIRS_PER_STEP, 1)
    blk = pl.BlockSpec((1, ch, pairs * LANES), lambda i, j, c: (i, c, j))
    vec = pl.BlockSpec((1, pairs * LANES), lambda i, j, c: (0, j))
    st = pl.BlockSpec((1, pairs, LANES, LANES), lambda i, j, c: (i, j, 0, 0))
    full = lambda s: pl.BlockSpec(s, lambda i, j, c: (0, 0))
    o, sf = pl.pallas_call(
        _rwkv_chunk_kernel,
        grid=(b, hp // pairs, nc),
        in_specs=[blk] * 7 + [vec, vec, st, full((ch, ch)), full((LANES, LANES))],
        out_specs=[blk, st],
        out_shape=[jax.ShapeDtypeStruct((b, tp, rww), BF16),
                   jax.ShapeDtypeStruct((b, hp, LANES, LANES), F32)],
        scratch_shapes=[pltpu.VMEM((pairs, LANES, LANES), F32)],
        compiler_params=_params(("parallel", "parallel", "arbitrary")),
        name="rwkv_chunk",
    )(r, lw, k, v, na, bb, bonus, ln_g.reshape(1, rww), ln_b.reshape(1, rww), s0, ltri, _head_ones())
    sf = sf.reshape(b, hp, 2, hd, 2, hd)
    wkv = jnp.stack([sf[:, :, 0, :, 0, :], sf[:, :, 1, :, 1, :]], axis=2).reshape(b, 2 * hp, hd, hd)
    return o[:, :t], wkv


def _suffix_matrix(tk):
    i = jnp.arange(tk)
    upper = (i[:, None] > i[None, :]).astype(BF16)
    return jnp.concatenate([upper, jnp.ones((tk, tk), BF16)], axis=1)


def _sb_prompt_kernel(bias_ref, q_ref, k_ref, v_ref, mxo_ref, o_ref, acc, carry, *, scale, hd, nh):
    qi = pl.program_id(2)
    tq = q_ref.shape[1]
    mxo = mxo_ref[...]

    def heads_of(ref, rows):
        return jnp.stack([ref[0, rows, i * hd:(i + 1) * hd] for i in range(nh)], axis=0)

    q = (heads_of(q_ref, slice(None)) * scale).astype(BF16)
    m = nh * tq

    def scores(rows):
        z = _dg(q, heads_of(k_ref, rows).astype(BF16), nt=True).reshape(m, rows.size)
        bias = jnp.concatenate([jnp.broadcast_to(bias_ref[i, 0:1, :rows.size], (tq, rows.size))
                                for i in range(nh)], axis=0)
        return _softplus_pair(z + bias)

    def weighted_values(att, rows):
        pv = _dg(att.astype(BF16).reshape(nh, tq, rows.size), heads_of(v_ref, rows).astype(BF16))
        return pv.reshape(m, hd)

    def rows_at(kb, n):
        return pl.ds(pl.multiple_of(kb * tq, tq), n)

    row = lax.broadcasted_iota(jnp.int32, (m, tq), 0) % tq
    col = lax.broadcasted_iota(jnp.int32, (m, tq), 1)
    mask = col < row
    rows = rows_at(qi, tq)
    sp, la = scores(rows)
    lsm = jnp.where(mask, -sp, 0.0)
    suf = _dg(lsm.astype(BF16), mxo)
    att = jnp.where(mask, jnp.exp(la + suf), 0.0)
    acc[...] = weighted_values(att, rows)
    carry[...] = jnp.broadcast_to(suf[:, 0:1] + lsm[:, 0:1], (m, hd))

    def wide(kb, c):
        rows = rows_at(kb - 1, 2 * tq)
        sp, la = scores(rows)
        lsm = -sp
        lsb = lsm.astype(BF16)
        suf = _dg(jnp.concatenate([lsb[:, :tq], lsb[:, tq:]], axis=0), mxo)
        suf_old, suf_new = suf[:m], suf[m:]
        tot_new = suf_new[:, 0:1] + lsm[:, tq:tq + 1]
        tot_old = suf_old[:, 0:1] + lsm[:, 0:1]
        att_old = jnp.exp(la[:, :tq] + suf_old + tot_new)
        att_new = jnp.exp(la[:, tq:] + suf_new)
        pv = weighted_values(jnp.concatenate([att_old, att_new], axis=1), rows)
        cr = carry[...]
        acc[...] += jnp.exp(cr) * pv
        carry[...] = cr + jnp.broadcast_to(tot_old + tot_new, (m, hd))
        return c

    lax.fori_loop(0, qi // 2, lambda i, c: wide(qi - 1 - 2 * i, c), 0)

    @pl.when(qi % 2 == 1)
    def _():
        rows = rows_at(0, tq)
        sp, la = scores(rows)
        att = jnp.exp(la + _dg((-sp).astype(BF16), mxo))
        acc[...] += jnp.exp(carry[...]) * weighted_values(att, rows)

    for i in range(nh):
        o_ref[0, :, i * hd:(i + 1) * hd] = acc[i * tq:(i + 1) * tq, :].astype(o_ref.dtype)


def _sb_prompt(proj, bias, q_off, k_off, v_off, heads, hd):
    b, t, _ = proj.shape
    tq = _tile(t, 256, LANES)
    nh = _tile(heads, SB_HEADS_PER_STEP, 1)
    w = nh * hd
    assert q_off % w == 0 and k_off % w == 0 and v_off % w == 0
    qb, kb, vb = q_off // w, k_off // w, v_off // w
    bias_rows = jnp.broadcast_to(bias.astype(F32)[:, None, None], (heads, 8, 2 * tq))
    return pl.pallas_call(
        functools.partial(_sb_prompt_kernel, scale=hd ** -0.5, hd=hd, nh=nh),
        grid=(b, heads // nh, t // tq),
        in_specs=[pl.BlockSpec((nh, 8, 2 * tq), lambda i, h, j: (h, 0, 0)),
                  pl.BlockSpec((1, tq, w), lambda i, h, j: (i, j, qb + h)),
                  pl.BlockSpec((1, t, w), lambda i, h, j: (i, 0, kb + h)),
                  pl.BlockSpec((1, t, w), lambda i, h, j: (i, 0, vb + h)),
                  pl.BlockSpec((tq, tq), lambda i, h, j: (0, 0))],
        out_specs=pl.BlockSpec((1, tq, w), lambda i, h, j: (i, j, h)),
        out_shape=jax.ShapeDtypeStruct((b, t, heads * hd), BF16),
        scratch_shapes=[pltpu.VMEM((nh * tq, hd), F32), pltpu.VMEM((nh * tq, hd), F32)],
        compiler_params=_params(("parallel", "parallel", "arbitrary")),
        name="sb_prompt",
    )(bias_rows, proj, proj, proj, _suffix_matrix(tq)[:, :tq])


SB_HEADS_PER_STEP = 2
SB_QPAD = 16
SB_PAGES_PER_STEP = 4


def _sb_sample_kernel(pt_ref, q_ref, *refs, heads, hd, tq, scale, pps, hg):
    ng = heads // hg
    group = lambda rs: [rs[i * ng:(i + 1) * ng] for i in range(len(rs) // ng)]
    kn_refs, vn_refs = group(refs[:ng]), group(refs[ng:2 * ng])
    refs = refs[2 * ng:]
    kp_refs, vp_refs = group(refs[:pps * ng]), group(refs[pps * ng:2 * pps * ng])
    bias_ref, mxo_ref, o_ref, zbuf, abuf, pvbuf, acc, carry = refs[2 * pps * ng:]
    j = pl.program_id(1)
    nj = pl.num_programs(1)
    qp = SB_QPAD
    rows = heads * qp

    def pages(k_refs, v_refs, masked):
        n = len(k_refs)
        tk = k_refs[0][0].shape[1]

        def head_rows(page_refs, h):
            ref = page_refs[h // hg].reshape(tk * hg, hd)
            return ref[pl.ds(h % hg, tk, stride=hg), :].astype(BF16)

        qs = []
        for h in range(heads):
            qh = jnp.concatenate([q_ref[0, :, h * hd:(h + 1) * hd] * scale,
                                  jnp.zeros((qp - tq, hd), F32)], axis=0)
            qs.append(qh.astype(BF16))
        for s in range(n):
            for h in range(heads):
                zbuf[s, h * qp:(h + 1) * qp, :] = _dg(qs[h], head_rows(k_refs[s], h), nt=True)
        sp, la = _softplus_pair(zbuf[0:n] + bias_ref[...])
        if masked:
            qidx = lax.broadcasted_iota(jnp.int32, (rows, tk), 0) % qp
            kidx = lax.broadcasted_iota(jnp.int32, (rows, tk), 1)
            mask = kidx < qidx
            lsm = jnp.where(mask, -sp, 0.0)
        else:
            lsm = -sp
        st = _dg(lsm.astype(BF16).reshape(n * rows, tk), mxo_ref[...]).reshape(n, rows, 2 * tk)
        att = jnp.exp(la + st[:, :, :tk])
        if masked:
            att = jnp.where(mask, att, 0.0)
        abuf[0:n] = att
        for s in range(n):
            for h in range(heads):
                ah = abuf[s, h * qp:(h + 1) * qp, :].astype(BF16)
                pvbuf[s, h * qp:(h + 1) * qp, :] = _dg(ah, head_rows(v_refs[s], h))
        return st[:, :, tk:tk + hd]

    @pl.when(j == 0)
    def _():
        carry[...] = pages(kn_refs, vn_refs, True)[0]
        acc[...] = pvbuf[0]

    @pl.when(j > 0)
    def _():
        tots = pages(kp_refs, vp_refs, False)
        cr = carry[...]
        out = acc[...]
        for s in range(pps):
            out = out + jnp.exp(cr) * pvbuf[s]
            cr = cr + tots[s]
        acc[...] = out
        carry[...] = cr

    @pl.when(j == nj - 1)
    def _():
        for h in range(heads):
            o_ref[0, :, h * hd:(h + 1) * hd] = acc[h * qp:h * qp + tq, :]


def _sb_sample(q, k_new, v_new, cache_k, cache_v, layer, page_table, bias):
    nb, tq, sbw = q.shape
    depth, nphys, page, heads, hd = cache_k.shape
    n_pages = page_table.shape[1]
    assert tq <= SB_QPAD and tq <= page and page % LANES == 0
    hg = 8 if heads % 8 == 0 else heads
    ng = heads // hg
    pad = lambda x: jnp.pad(x.reshape(nb, tq, heads, hd), ((0, 0), (0, page - tq), (0, 0), (0, 0)))
    ck = cache_k.reshape(depth * nphys, page, heads, hd)
    cv = cache_v.reshape(depth * nphys, page, heads, hd)
    page_table = page_table + layer * nphys
    rows = heads * SB_QPAD
    bias_rows = jnp.broadcast_to(jnp.repeat(bias.astype(F32), SB_QPAD)[:, None], (rows, page))
    pps = _tile(n_pages, SB_PAGES_PER_STEP, 1)

    def page_map(s, g):
        def index(i, j, pt):
            return (pt[i * n_pages + n_pages - 1 - (jnp.maximum(j, 1) - 1) * pps - s], 0, g, 0)
        return index

    const = lambda i, j, pt: (0, 0)
    blk = (1, page, hg, hd)
    new = [pl.BlockSpec(blk, lambda i, j, pt, g=g: (i, 0, g, 0)) for g in range(ng)]
    pages = [pl.BlockSpec(blk, page_map(s, g)) for s in range(pps) for g in range(ng)]
    return pl.pallas_call(
        functools.partial(_sb_sample_kernel, heads=heads, hd=hd, tq=tq, scale=hd ** -0.5, pps=pps, hg=hg),
        grid_spec=pltpu.PrefetchScalarGridSpec(
            num_scalar_prefetch=1,
            grid=(nb, n_pages // pps + 1),
            in_specs=[pl.BlockSpec((1, tq, sbw), lambda i, j, pt: (i, 0, 0))] + new + new + pages + pages +
                     [pl.BlockSpec((rows, page), const),
                      pl.BlockSpec((page, 2 * page), const)],
            out_specs=pl.BlockSpec((1, tq, sbw), lambda i, j, pt: (i, 0, 0)),
            scratch_shapes=[pltpu.VMEM((pps, rows, page), F32), pltpu.VMEM((pps, rows, page), F32),
                            pltpu.VMEM((pps, rows, hd), F32),
                            pltpu.VMEM((rows, hd), F32), pltpu.VMEM((rows, hd), F32)]),
        out_shape=jax.ShapeDtypeStruct((nb, tq, sbw), F32),
        compiler_params=_params(("parallel", "arbitrary")),
        name="sb_sample",
    )(page_table.reshape(-1).astype(jnp.int32), q, *([pad(k_new)] * ng), *([pad(v_new)] * ng),
      *([ck] * (pps * ng)), *([cv] * (pps * ng)), bias_rows, _suffix_matrix(page))


def _router_kernel(x_ref, g_ref, rw_ref, rb_ref, h_ref, id_ref, wt_ref, *, n_groups, per_group):
    x = x_ref[...]
    ms = jnp.mean(x * x, axis=-1, keepdims=True)
    h = x * lax.rsqrt(ms + RMS_EPS) * g_ref[...]
    h_ref[...] = h
    logits = _dot3(h, rw_ref[...]) + rb_ref[...]
    tm = logits.shape[0]
    lane = lax.broadcasted_iota(jnp.int32, (tm, LANES), 1)
    lane_f = lane.astype(F32)
    neg = jnp.finfo(F32).min
    far = float(LANES)
    is_g = lane < n_groups
    gl = jnp.where(is_g, logits, neg)
    gmax = jnp.max(gl, axis=1, keepdims=True)
    gidx = jnp.min(jnp.where(gl == gmax, lane_f, far), axis=1, keepdims=True).astype(jnp.int32)
    gprob = 1.0 / jnp.sum(jnp.where(is_g, jnp.exp(logits - gmax), 0.0), axis=1, keepdims=True)
    lo = n_groups + gidx * per_group
    el = jnp.where((lane >= lo) & (lane < lo + per_group), logits, neg)
    v1 = jnp.max(el, axis=1, keepdims=True)
    i1 = jnp.min(jnp.where(el == v1, lane_f, far), axis=1, keepdims=True).astype(jnp.int32)
    el2 = jnp.where(lane == i1, neg, el)
    v2 = jnp.max(el2, axis=1, keepdims=True)
    i2 = jnp.min(jnp.where((el2 == v2) & (lane != i1), lane_f, far), axis=1, keepdims=True).astype(jnp.int32)
    t = jnp.exp(v2 - v1)
    w1 = gprob / (1.0 + t)
    w2 = gprob * t / (1.0 + t)
    id_ref[...] = jnp.where(lane == 0, i1 - n_groups, jnp.where(lane == 1, i2 - n_groups, 0))
    wt_ref[...] = jnp.where(lane == 0, w1, jnp.where(lane == 1, w2, 0.0))


def _router(x, g, rg_w, rg_b, re_w, re_b):
    m, d = x.shape
    n_groups = rg_w.shape[1]
    n_exp = re_w.shape[1]
    assert n_groups + n_exp <= LANES
    padc = LANES - n_groups - n_exp
    rw = jnp.concatenate([rg_w, re_w, jnp.zeros((d, padc), F32)], axis=1).astype(F32)
    rb = jnp.concatenate([rg_b, re_b, jnp.zeros((padc,), F32)]).reshape(1, LANES).astype(F32)
    tm = _tile(m, 256)
    row = pl.BlockSpec((tm, d), lambda i: (i, 0))
    nar = pl.BlockSpec((tm, LANES), lambda i: (i, 0))
    h, ids, wts = pl.pallas_call(
        functools.partial(_router_kernel, n_groups=n_groups, per_group=n_exp // n_groups),
        grid=(m // tm,),
        in_specs=[row, pl.BlockSpec((1, d), lambda i: (0, 0)),
                  pl.BlockSpec((d, LANES), lambda i: (0, 0)),
                  pl.BlockSpec((1, LANES), lambda i: (0, 0))],
        out_specs=[row, nar, nar],
        out_shape=[jax.ShapeDtypeStruct((m, d), F32),
                   jax.ShapeDtypeStruct((m, LANES), jnp.int32),
                   jax.ShapeDtypeStruct((m, LANES), F32)],
        compiler_params=_params(("parallel",)),
        name="router",
    )(x, g.reshape(1, d), rw, rb)
    return h, ids[:, :TOP_K], wts


def _moe_kernel(ue_ref, us_ref, un_ref, tok_ref, h_hbm, wg_ref, wu_ref, wd_ref, y_ref,
                xf, xb, wgb, wub, wdb, sem):
    u = pl.program_id(0)
    f = pl.program_id(1)
    n = un_ref[u]

    def row_copy(tok, i):
        return pltpu.make_async_copy(h_hbm.at[pl.ds(tok, 1), :], xf.at[pl.ds(i, 1), :], sem)

    @pl.when((f == 0) & (n > 0))
    def _():
        xf[...] = jnp.zeros_like(xf)
        base = us_ref[u]

        def start(i, c):
            row_copy(tok_ref[base + i], i).start()
            return c

        def wait(i, c):
            row_copy(0, i).wait()
            return c

        lax.fori_loop(0, n, start, 0)
        lax.fori_loop(0, n, wait, 0)
        xb[...] = xf[...].astype(BF16)

    @pl.when((f == 0) & (n == 0))
    def _():
        y_ref[...] = jnp.zeros_like(y_ref)

    @pl.when(n > 0)
    def _():
        wgb[...] = wg_ref[0].astype(BF16)
        wub[...] = wu_ref[0].astype(BF16)
        wdb[...] = wd_ref[0].astype(BF16)
        n_sub = (n + (MOE_SUB - 1)) // MOE_SUB
        for k in range(1, MOE_UNIT // MOE_SUB + 1):
            m_rows = k * MOE_SUB

            @pl.when(n_sub == k)
            def _():
                x = xb[0:m_rows, :]
                g = _dg(x, wgb[...])
                up = _dg(x, wub[...])
                hid = (g * _sigmoid(g) * up).astype(BF16)
                d = y_ref.shape[1]
                tn = _tile(d, MOE_DOWN_COLS, LANES)

                @pl.when(f == 0)
                def _():
                    for c0 in range(0, d, tn):
                        y_ref[0:m_rows, c0:c0 + tn] = _dg(hid, wdb[:, c0:c0 + tn])
                    if m_rows < MOE_UNIT:
                        y_ref[m_rows:MOE_UNIT, :] = jnp.zeros((MOE_UNIT - m_rows, d), F32)

                @pl.when(f > 0)
                def _():
                    for c0 in range(0, d, tn):
                        y_ref[0:m_rows, c0:c0 + tn] += _dg(hid, wdb[:, c0:c0 + tn])


def _moe_experts(h, tok_sorted, unit_e, unit_start, unit_n, w_gate, w_up, w_down):
    m, d = h.shape
    n_units = unit_e.shape[0]
    de = w_gate.shape[2]
    tf = _tile(de, 256, LANES)
    nf = de // tf

    def fidx(u, f, un):
        return jnp.where(un[u] > 0, f, nf - 1)

    return pl.pallas_call(
        _moe_kernel,
        grid_spec=pltpu.PrefetchScalarGridSpec(
            num_scalar_prefetch=4,
            grid=(n_units, nf),
            in_specs=[pl.BlockSpec(memory_space=pl.ANY),
                      pl.BlockSpec((1, d, tf), lambda u, f, ue, us, un, tok: (ue[u], 0, fidx(u, f, un))),
                      pl.BlockSpec((1, d, tf), lambda u, f, ue, us, un, tok: (ue[u], 0, fidx(u, f, un))),
                      pl.BlockSpec((1, tf, d), lambda u, f, ue, us, un, tok: (ue[u], fidx(u, f, un), 0))],
            out_specs=pl.BlockSpec((MOE_UNIT, d), lambda u, f, ue, us, un, tok: (u, 0)),
            scratch_shapes=[pltpu.VMEM((MOE_UNIT, d), F32), pltpu.VMEM((MOE_UNIT, d), BF16),
                            pltpu.VMEM((d, tf), BF16), pltpu.VMEM((d, tf), BF16),
                            pltpu.VMEM((tf, d), BF16), pltpu.SemaphoreType.DMA(())]),
        out_shape=jax.ShapeDtypeStruct((n_units * MOE_UNIT, d), F32),
        compiler_params=_params(("arbitrary", "arbitrary")),
        name="moe_experts",
    )(unit_e, unit_start, unit_n, tok_sorted, h, w_gate, w_up, w_down)


def _combine_kernel(pos_ref, ys_hbm, w_ref, x_ref, g_ref, o_ref, buf, sem, *, normalize):
    i = pl.program_id(0)
    tc = x_ref.shape[0]
    base = i * tc * TOP_K

    def row_copy(p, s, t):
        return pltpu.make_async_copy(ys_hbm.at[pl.ds(p, 1), :], buf.at[s, pl.ds(t, 1), :], sem)

    def start(t, c):
        for s in range(TOP_K):
            row_copy(pos_ref[base + TOP_K * t + s], s, t).start()
        return c

    def wait(t, c):
        for s in range(TOP_K):
            row_copy(0, s, t).wait()
        return c

    lax.fori_loop(0, tc, start, 0)
    lax.fori_loop(0, tc, wait, 0)
    w = w_ref[...]
    x = x_ref[...] + (w[:, 0:1] * buf[0] + w[:, 1:2] * buf[1])
    if normalize:
        ms = jnp.mean(x * x, axis=-1, keepdims=True)
        x = x * lax.rsqrt(ms + RMS_EPS) * g_ref[...]
    o_ref[...] = x


def _moe_combine(ys, pos, wts, x, g, normalize):
    m, d = x.shape
    tc = _tile(m, 128)
    return pl.pallas_call(
        functools.partial(_combine_kernel, normalize=normalize),
        grid_spec=pltpu.PrefetchScalarGridSpec(
            num_scalar_prefetch=1,
            grid=(m // tc,),
            in_specs=[pl.BlockSpec(memory_space=pl.ANY),
                      pl.BlockSpec((tc, LANES), lambda i, p: (i, 0)),
                      pl.BlockSpec((tc, d), lambda i, p: (i, 0)),
                      pl.BlockSpec((1, d), lambda i, p: (0, 0))],
            out_specs=pl.BlockSpec((tc, d), lambda i, p: (i, 0)),
            scratch_shapes=[pltpu.VMEM((TOP_K, tc, d), F32), pltpu.SemaphoreType.DMA(())]),
        out_shape=jax.ShapeDtypeStruct((m, d), F32),
        compiler_params=_params(("arbitrary",)),
        name="moe_combine",
    )(pos, ys, wts, x, g.reshape(1, d))


def _moe_layer(x, norm_g, rg_w, rg_b, re_w, re_b, w_gate, w_up, w_down, final_g, normalize):
    m, d = x.shape
    n_exp = re_w.shape[1]
    h, expert, wts = _router(x, norm_g, rg_w, rg_b, re_w, re_b)

    n_rows = m * TOP_K
    flat_e = expert.reshape(-1)
    order = jnp.argsort(flat_e).astype(jnp.int32)
    e_sorted = flat_e[order]
    tok_sorted = order // TOP_K
    counts = jnp.zeros((n_exp,), jnp.int32).at[flat_e].add(1)
    start = jnp.cumsum(counts) - counts
    units_e = (counts + MOE_UNIT - 1) // MOE_UNIT
    ucum = jnp.cumsum(units_e)
    ufirst = ucum - units_e
    n_units = n_exp + -(-n_rows // MOE_UNIT)
    uid = jnp.arange(n_units, dtype=jnp.int32)
    total = ucum[-1]
    ue = jnp.minimum(jnp.searchsorted(ucum, uid, side="right"), n_exp - 1).astype(jnp.int32)
    local = uid - ufirst[ue]
    active = uid < total
    un = jnp.where(active, jnp.clip(counts[ue] - local * MOE_UNIT, 0, MOE_UNIT), 0).astype(jnp.int32)
    us = jnp.where(active, start[ue] + local * MOE_UNIT, 0).astype(jnp.int32)
    last_e = ue[jnp.maximum(total - 1, 0)]
    ue = jnp.where(active, ue, last_e).astype(jnp.int32)
    local_row = jnp.arange(n_rows, dtype=jnp.int32) - start[e_sorted]
    pos_sorted = (ufirst[e_sorted] + local_row // MOE_UNIT) * MOE_UNIT + local_row % MOE_UNIT
    pos = jnp.zeros((n_rows,), jnp.int32).at[order].set(pos_sorted.astype(jnp.int32))

    ys = _moe_experts(h, tok_sorted, ue, us, un, w_gate, w_up, w_down)
    return _moe_combine(ys, pos, wts, x, final_g, normalize)


def _mixer(x, shift0, wkv0, past, lw):
    b, t, d = x.shape
    rw_proj = lw["mu"].shape[0]
    rww = lw["w_up_a"].shape[0]
    sbw = lw["w_up_b"].shape[0]
    heads = lw["sb_bias"].shape[0]
    hd = sbw // heads
    q_off, k_off, v_off = rw_proj, rw_proj + sbw, rw_proj + 2 * sbw
    ga_off, gb_off = rw_proj + 3 * sbw, rw_proj + 3 * sbw + d
    x2 = x.reshape(b * t, d)

    h = _rmsnorm(x2, lw["norm1_g"], BF16)
    proj = _matmul(h, lw["w_in"], F32, 1024, 768)
    proj3 = proj.reshape(b, t, -1)

    r, lgw, k2, v, na, bb, bonus = _rwkv_prep(proj3, shift0, lw["mu"], lw["w0"], lw["w2"], lw["a0"],
                                              lw["a2"], lw["kk"], lw["ka"], lw["rk"])
    o_a, wkv_new = _rwkv_recurrence(r, lgw, k2, v, na, bb, bonus, wkv0, lw["ln_g"], lw["ln_b"])
    shift_new = proj3[:, -1, :rw_proj]

    k_new = proj3[:, :, k_off:k_off + sbw]
    v_new = proj3[:, :, v_off:v_off + sbw]
    if past is None:
        o_b = _sb_prompt(proj3, lw["sb_bias"], q_off, k_off, v_off, heads, hd)
    else:
        cache_k, cache_v, layer, page_table = past
        q = proj3[:, :, q_off:q_off + sbw]
        o_b = _sb_sample(q, k_new, v_new, cache_k, cache_v, layer, page_table, lw["sb_bias"]).astype(BF16)

    mixed = _merge_up(o_a.reshape(b * t, rww), o_b.reshape(b * t, sbw), proj, lw["w_up_a"], lw["w_up_b"],
                      ga_off, gb_off)
    x1 = _out_proj(mixed, lw["w_out"], x2)
    return (x1, k_new.reshape(b, t, heads, hd), v_new.reshape(b, t, heads, hd), wkv_new, shift_new)


def kernel(x_prompt, x_sample, cache_k, cache_v, state_wkv, state_shift, page_table, norm1_g, w_in, rwkv_mu, rwkv_w0, rwkv_w2, rwkv_a0, rwkv_a2, rwkv_kk, rwkv_ka, rwkv_rk, rwkv_ln_g, rwkv_ln_b, sb_bias, w_up_a, w_up_b, w_out, norm2_g, router_group_w, router_group_b, router_expert_w, router_expert_b, expert_w_gate, expert_w_up, expert_w_down, normf_g):
    depth = w_in.shape[0]
    bp, tp, d = x_prompt.shape
    bs, ts, _ = x_sample.shape
    rw_proj = rwkv_mu.shape[1]
    rww = w_up_a.shape[1]
    rw_heads = rww // RW_HEAD_DIM
    xp, xs = x_prompt, x_sample
    outs = [[] for _ in range(8)]
    for l in range(depth):
        lw = dict(norm1_g=norm1_g[l], w_in=w_in[l].astype(BF16), mu=rwkv_mu[l], w0=rwkv_w0[l],
                  w2=rwkv_w2[l], a0=rwkv_a0[l], a2=rwkv_a2[l], kk=rwkv_kk[l], ka=rwkv_ka[l],
                  rk=rwkv_rk[l].reshape(-1), ln_g=rwkv_ln_g[l], ln_b=rwkv_ln_b[l], sb_bias=sb_bias[l],
                  w_up_a=w_up_a[l].astype(BF16), w_up_b=w_up_b[l].astype(BF16),
                  w_out=w_out[l].astype(BF16))
        shift0 = jnp.zeros((bp, rw_proj), F32)
        wkv0 = jnp.zeros((bp, rw_heads, RW_HEAD_DIM, RW_HEAD_DIM), F32)
        res_p = _mixer(xp, shift0, wkv0, None, lw)
        res_s = _mixer(xs, state_shift[l], state_wkv[l], (cache_k, cache_v, l, page_table), lw)
        for i in range(4):
            outs[i].append(res_p[1 + i])
            outs[4 + i].append(res_s[1 + i])
        x1 = jnp.concatenate([res_p[0], res_s[0]], axis=0)
        last = l == depth - 1
        x2 = _moe_layer(x1, norm2_g[l], router_group_w[l], router_group_b[l], router_expert_w[l],
                        router_expert_b[l], expert_w_gate[l], expert_w_up[l], expert_w_down[l],
                        normf_g, last)
        xp = x2[:bp * tp].reshape(bp, tp, d)
        xs = x2[bp * tp:].reshape(bs, ts, d)
    return (xp, xs) + tuple(jnp.stack(o) for o in outs)
```

```python
import functools

import jax
import jax.numpy as jnp
from jax import lax
from jax.experimental import pallas as pl
from jax.experimental.pallas import tpu as pltpu

F32 = jnp.float32
BF16 = jnp.bfloat16

LANES = 128
RW_HEAD_DIM = 64
RW_CHUNK = 64
RW_INV_BLOCK = 16
RW_PAIRS_PER_STEP = 8
RW_GN_EPS = 64e-5
RMS_EPS = 1e-6
TOP_K = 2
MOE_SUB = 128
MOE_UNIT = 3 * MOE_SUB
MOE_DOWN_COLS = 512
VMEM_LIMIT_MB = 56


def _params(sem, vmem_mb=VMEM_LIMIT_MB):
    return pltpu.CompilerParams(dimension_semantics=sem, vmem_limit_bytes=vmem_mb << 20)


def _tile(n, pref, mult=8):
    t = min(pref, n)
    while t >= mult:
        if n % t == 0 and t % mult == 0:
            return t
        t -= 1
    return n


def _dg(a, b, nt=False):
    if a.ndim == 3:
        dims = (((2,), (2 if nt else 1,)), ((0,), (0,)))
    else:
        dims = (((1,), (1 if nt else 0,)), ((), ()))
    return lax.dot_general(a, b, dims, preferred_element_type=F32)


def _split(x):
    hi = x.astype(BF16)
    lo = (x - hi.astype(F32)).astype(BF16)
    return hi, lo


def _dot1(a, b, nt=False):
    return _dg(a.astype(BF16), b.astype(BF16), nt)


def _dot3(a, b, nt=False):
    ah, al = _split(a)
    bh, bl = _split(b)
    return _dg(ah, bh, nt) + (_dg(ah, bl, nt) + _dg(al, bh, nt))


def _dot2x(a, b_exact):
    ah, al = _split(a)
    return _dg(ah, b_exact) + _dg(al, b_exact)


def _sigmoid(x):
    return 1.0 / (1.0 + jnp.exp(-x))


def _softplus(x):
    return jnp.maximum(x, 0.0) + jnp.log1p(jnp.exp(-jnp.abs(x)))


def _softplus_pair(z):
    l = jnp.log(1.0 + jnp.exp(-jnp.abs(z)))
    return jnp.maximum(z, 0.0) + l, jnp.minimum(z, 0.0) - l


def _rmsnorm_kernel(x_ref, g_ref, o_ref):
    x = x_ref[...]
    ms = jnp.mean(x * x, axis=-1, keepdims=True)
    o_ref[...] = (x * lax.rsqrt(ms + RMS_EPS) * g_ref[...]).astype(o_ref.dtype)


def _rmsnorm(x, g, out_dtype):
    m, d = x.shape
    tm = _tile(m, 256)
    return pl.pallas_call(
        _rmsnorm_kernel,
        grid=(m // tm,),
        in_specs=[pl.BlockSpec((tm, d), lambda i: (i, 0)),
                  pl.BlockSpec((1, d), lambda i: (0, 0))],
        out_specs=pl.BlockSpec((tm, d), lambda i: (i, 0)),
        out_shape=jax.ShapeDtypeStruct((m, d), out_dtype),
        compiler_params=_params(("parallel",)),
        name="rmsnorm",
    )(x, g.reshape(1, d))


def _mm_kernel(a_ref, b_ref, o_ref):
    o_ref[...] = jnp.dot(a_ref[...], b_ref[...], preferred_element_type=F32).astype(o_ref.dtype)


def _matmul(a, b, out_dtype, tm_pref, tn_pref):
    m, k = a.shape
    n = b.shape[1]
    tm = _tile(m, tm_pref, 16)
    tn = _tile(n, tn_pref, LANES)
    return pl.pallas_call(
        _mm_kernel,
        grid=(m // tm, n // tn),
        in_specs=[pl.BlockSpec((tm, k), lambda i, j: (i, 0)),
                  pl.BlockSpec((k, tn), lambda i, j: (0, j))],
        out_specs=pl.BlockSpec((tm, tn), lambda i, j: (i, j)),
        out_shape=jax.ShapeDtypeStruct((m, n), out_dtype),
        compiler_params=_params(("parallel", "arbitrary")),
        name="in_proj",
    )(a, b)


def _merge_kernel(oa_ref, ob_ref, wa_ref, wb_ref, ga_ref, gb_ref, o_ref):
    ya = jnp.dot(oa_ref[...], wa_ref[...], preferred_element_type=F32)
    yb = jnp.dot(ob_ref[...], wb_ref[...], preferred_element_type=F32)
    o_ref[...] = (_sigmoid(ga_ref[...]) * ya + _sigmoid(gb_ref[...]) * yb).astype(o_ref.dtype)


def _merge_up(o_a, o_b, proj, w_up_a, w_up_b, ga_off, gb_off):
    m, ka = o_a.shape
    kb = o_b.shape[1]
    d = w_up_a.shape[1]
    tm = _tile(m, 1024, 16)
    tn = 256
    assert d % tn == 0 and ga_off % tn == 0 and gb_off % tn == 0
    ga_blk, gb_blk = ga_off // tn, gb_off // tn
    return pl.pallas_call(
        _merge_kernel,
        grid=(m // tm, d // tn),
        in_specs=[pl.BlockSpec((tm, ka), lambda i, j: (i, 0)),
                  pl.BlockSpec((tm, kb), lambda i, j: (i, 0)),
                  pl.BlockSpec((ka, tn), lambda i, j: (0, j)),
                  pl.BlockSpec((kb, tn), lambda i, j: (0, j)),
                  pl.BlockSpec((tm, tn), lambda i, j: (i, ga_blk + j)),
                  pl.BlockSpec((tm, tn), lambda i, j: (i, gb_blk + j))],
        out_specs=pl.BlockSpec((tm, tn), lambda i, j: (i, j)),
        out_shape=jax.ShapeDtypeStruct((m, d), BF16),
        compiler_params=_params(("parallel", "arbitrary")),
        name="merge_up",
    )(o_a, o_b, w_up_a, w_up_b, proj, proj)


def _outproj_kernel(a_ref, b_ref, x_ref, o_ref):
    o_ref[...] = x_ref[...] + jnp.dot(a_ref[...], b_ref[...], preferred_element_type=F32)


def _out_proj(mixed, w_out, x):
    m, k = mixed.shape
    n = w_out.shape[1]
    tm = _tile(m, 1024, 16)
    tn = _tile(n, 512, LANES)
    return pl.pallas_call(
        _outproj_kernel,
        grid=(m // tm, n // tn),
        in_specs=[pl.BlockSpec((tm, k), lambda i, j: (i, 0)),
                  pl.BlockSpec((k, tn), lambda i, j: (0, j)),
                  pl.BlockSpec((tm, tn), lambda i, j: (i, j))],
        out_specs=pl.BlockSpec((tm, tn), lambda i, j: (i, j)),
        out_shape=jax.ShapeDtypeStruct((m, n), F32),
        compiler_params=_params(("parallel", "arbitrary")),
        name="out_proj",
    )(mixed, w_out, x)


def _rwkv_prep_kernel(p_ref, sh_ref, mu_ref, w0_ref, w2_ref, a0_ref, a2_ref, kkw_ref, kaw_ref,
                      rkw_ref, ones_ref,
                      r_out, lw_out, k_out, v_out, na_out, bb_out, bon_out,
                      carry, *, rww, dl, al):
    t = pl.program_id(1)
    tt = p_ref.shape[1]

    @pl.when(t == 0)
    def _():
        carry[0:1, :] = sh_ref[0]

    row0 = lax.broadcasted_iota(jnp.int32, (tt, 1), 0) == 0

    def mixed(lo, hi):
        p = p_ref[0, :, lo:hi]
        prev = jnp.where(row0, carry[0:1, lo:hi], pltpu.roll(p, 1, 0))
        return p + (prev - p) * mu_ref[:, lo:hi]

    wd = mixed(3 * rww, 3 * rww + dl)
    ad = mixed(3 * rww + dl, 3 * rww + dl + al)
    w_lora = _dot3(jnp.tanh(wd), w2_ref[...])
    a_lora = _dot3(ad, a2_ref[...])
    ones = ones_ref[...]
    for s in range(rww // LANES):
        c0, c1 = s * LANES, (s + 1) * LANES
        r = mixed(c0, c1)
        k = mixed(rww + c0, rww + c1)
        v = mixed(2 * rww + c0, 2 * rww + c1)
        w = w0_ref[:, c0:c1] + w_lora[:, c0:c1]
        lw = -jnp.exp(-_softplus(-w) - 0.5)
        a = _sigmoid(a0_ref[:, c0:c1] + a_lora[:, c0:c1])
        kk = k * kkw_ref[:, c0:c1]
        ss = _dot2x(kk * kk, ones)
        kk = kk / jnp.maximum(jnp.sqrt(ss), 1e-12)
        k2 = k * (1.0 + (a - 1.0) * kaw_ref[:, c0:c1])
        rk = _dot2x(r * k2 * rkw_ref[:, c0:c1], ones)
        r_out[0, :, c0:c1] = r
        lw_out[0, :, c0:c1] = lw
        k_out[0, :, c0:c1] = k2
        v_out[0, :, c0:c1] = v
        na_out[0, :, c0:c1] = -kk
        bb_out[0, :, c0:c1] = kk * a
        bon_out[0, :, c0:c1] = rk * v
    carry[0:1, :] = p_ref[0, tt - 1:tt, :]


def _head_ones():
    i = jnp.arange(LANES) // RW_HEAD_DIM
    return (i[:, None] == i[None, :]).astype(BF16)


def _rwkv_prep(proj, shift0, mu, w0, w2, a0, a2, kkw, kaw, rkw):
    b, t, _ = proj.shape
    p = mu.shape[0]
    rww = w0.shape[0]
    dl, al = w2.shape[0], a2.shape[0]
    tt = _tile(t, 128)
    row = lambda x: x.reshape(1, -1)
    out = jax.ShapeDtypeStruct((b, t, rww), F32)
    vec = lambda n: pl.BlockSpec((1, n), lambda i, j: (0, 0))
    full = lambda s: pl.BlockSpec(s, lambda i, j: (0, 0))
    blk = pl.BlockSpec((1, tt, rww), lambda i, j: (i, j, 0))
    return pl.pallas_call(
        functools.partial(_rwkv_prep_kernel, rww=rww, dl=dl, al=al),
        grid=(b, t // tt),
        in_specs=[pl.BlockSpec((1, tt, p), lambda i, j: (i, j, 0)),
                  pl.BlockSpec((1, 1, p), lambda i, j: (i, 0, 0)),
                  vec(p), vec(rww), full((dl, rww)), vec(rww), full((al, rww)),
                  vec(rww), vec(rww), vec(rww), full((LANES, LANES))],
        out_specs=[blk] * 7,
        out_shape=[out] * 7,
        scratch_shapes=[pltpu.VMEM((8, p), F32)],
        compiler_params=_params(("parallel", "arbitrary")),
        name="rwkv_prep",
    )(proj, shift0.reshape(b, 1, p), row(mu), row(w0), w2, row(a0), a2, row(kkw), row(kaw),
      row(rkw), _head_ones())


def _tri_inv(a, n, nb):
    row = lax.broadcasted_iota(jnp.int32, (n, n), 0)
    col = lax.broadcasted_iota(jnp.int32, (n, n), 1)
    dmask = (row // RW_INV_BLOCK) == (col // RW_INV_BLOCK)
    eye = jnp.where(row == col, 1.0, 0.0)
    ad = jnp.where(dmask, a, 0.0)
    ao = jnp.where(dmask, 0.0, a)
    dinv = eye + ad
    p = ad
    k = 1
    while 2 * k < RW_INV_BLOCK:
        p = _dot1(p, p)
        dinv = dinv + _dot1(dinv, p)
        k *= 2
    bm = _dot1(dinv, ao)
    x = eye + bm
    p = bm
    k = 1
    while 2 * k < nb:
        p = _dot1(p, p)
        x = x + _dot1(x, p)
        k *= 2
    return _dot1(x, dinv)


def _rwkv_chunk_math(r, cum, lw, kk, v, na, bb, s):
    ch = r.shape[1]
    n = 2 * ch
    cl = cum[:, ch - 1:ch, :]
    e_neg = jnp.exp(-cum)
    e_end = jnp.exp(cl - cum)
    a_t = na * jnp.exp(cum - lw)
    r_t = r * jnp.exp(cum)
    lane_lo = lax.broadcasted_iota(jnp.int32, (ch, LANES), 1) < RW_HEAD_DIM

    def hat(x):
        return jnp.concatenate([jnp.where(lane_lo, x, 0.0), jnp.where(lane_lo, 0.0, x)], axis=1)

    ath, rth, vh = hat(a_t), hat(r_t), hat(v)
    btkt = jnp.concatenate([hat(bb * e_neg), hat(kk * e_neg)], axis=1)
    bbh, kbh = hat(bb * e_end), hat(kk * e_end)

    row = lax.broadcasted_iota(jnp.int32, (n, n), 0)
    col = lax.broadcasted_iota(jnp.int32, (n, n), 1)
    strict = row > col
    incl = row >= col
    sc = _dot1(jnp.concatenate([ath, rth], axis=1), btkt, nt=True)
    a_ab = jnp.where(strict, sc[:, :n, :n], 0.0)
    a_ak = jnp.where(strict, sc[:, :n, n:], 0.0)
    a_rb = jnp.where(incl, sc[:, n:, :n], 0.0)
    a_rk = jnp.where(incl, sc[:, n:, n:], 0.0)

    tinv = _tri_inv(a_ab, n, ch // RW_INV_BLOCK)
    x = _dot1(tinv, jnp.concatenate([_dot1(a_ak, vh), ath], axis=2))
    u0, a2 = x[:, :, :LANES], x[:, :, LANES:]
    z = _dot1(a_rb, x)
    y0 = z[:, :, :LANES] + _dot1(a_rk, vh)
    rh = rth + z[:, :, LANES:]
    yh = _dot1(rh, s, nt=True) + y0
    y = yh[:, :ch] + yh[:, ch:]

    tr = lambda t: jnp.swapaxes(t, 1, 2)
    m = jnp.where(row == col, jnp.exp(cl), 0.0) + _dot1(tr(a2), bbh)
    nn = _dot1(jnp.concatenate([tr(u0), tr(vh)], axis=2), jnp.concatenate([bbh, kbh], axis=1))
    return y, _dot3(s, m) + nn


def _rwkv_chunk_kernel(r_ref, lw_ref, k_ref, v_ref, na_ref, bb_ref, bon_ref, lng_ref, lnb_ref,
                       s0_ref, ltri_ref, ones_ref, o_ref, sf_ref, s_scr):
    c = pl.program_id(2)
    pairs = s_scr.shape[0]
    ch = r_ref.shape[1]

    @pl.when(c == 0)
    def _():
        s_scr[...] = s0_ref[0]

    def pairwise(x):
        return jnp.stack([x[:, g * LANES:(g + 1) * LANES] for g in range(pairs)], axis=0)

    lw = lw_ref[0]
    cum = _dot2x_lhs(ltri_ref[...], lw)
    y, s_new = _rwkv_chunk_math(pairwise(r_ref[0]), pairwise(cum), pairwise(lw), pairwise(k_ref[0]),
                                pairwise(v_ref[0]), pairwise(na_ref[0]), pairwise(bb_ref[0]), s_scr[...])
    s_scr[...] = s_new
    sf_ref[0] = s_new

    ones = ones_ref[...]
    y2 = y.reshape(pairs * ch, LANES)
    mean = _dot2x(y2, ones) * (1.0 / RW_HEAD_DIM)
    d = y2 - mean
    var = _dot2x(d * d, ones) * (1.0 / RW_HEAD_DIM)
    yn = d * lax.rsqrt(var + RW_GN_EPS)
    for g in range(pairs):
        sl = slice(g * LANES, (g + 1) * LANES)
        out = yn[g * ch:(g + 1) * ch] * lng_ref[:, sl] + lnb_ref[:, sl] + bon_ref[0, :, sl]
        o_ref[0, :, sl] = out.astype(o_ref.dtype)


def _dot2x_lhs(a_exact, b):
    bh, bl = _split(b)
    return _dg(a_exact, bh) + _dg(a_exact, bl)


def _rwkv_recurrence(r, lw, k, v, na, bb, bonus, wkv0, ln_g, ln_b):
    b, t, rww = r.shape
    hp = rww // LANES
    hd = RW_HEAD_DIM
    ch = RW_CHUNK
    tp = -(-t // ch) * ch
    if tp != t:
        pad = lambda x: jnp.pad(x, ((0, 0), (0, tp - t), (0, 0)))
        r, lw, k, v, na, bb, bonus = map(pad, (r, lw, k, v, na, bb, bonus))
    nc = tp // ch
    w0 = wkv0.astype(F32).reshape(b, hp, 2, hd, hd)
    zero = jnp.zeros((b, hp, hd, hd), F32)
    s0 = jnp.concatenate([jnp.concatenate([w0[:, :, 0], zero], axis=-1),
                          jnp.concatenate([zero, w0[:, :, 1]], axis=-1)], axis=-2)
    ltri = (jnp.arange(ch)[:, None] >= jnp.arange(ch)[None, :]).astype(BF16)
    pairs = _tile(hp, RW_PAIRS_PER_STEP, 1)
    blk = pl.BlockSpec((1, ch, pairs * LANES), lambda i, j, c: (i, c, j))
    vec = pl.BlockSpec((1, pairs * LANES), lambda i, j, c: (0, j))
    st = pl.BlockSpec((1, pairs, LANES, LANES), lambda i, j, c: (i, j, 0, 0))
    full = lambda s: pl.BlockSpec(s, lambda i, j, c: (0, 0))
    o, sf = pl.pallas_call(
        _rwkv_chunk_kernel,
        grid=(b, hp // pairs, nc),
        in_specs=[blk] * 7 + [vec, vec, st, full((ch, ch)), full((LANES, LANES))],
        out_specs=[blk, st],
        out_shape=[jax.ShapeDtypeStruct((b, tp, rww), BF16),
                   jax.ShapeDtypeStruct((b, hp, LANES, LANES), F32)],
        scratch_shapes=[pltpu.VMEM((pairs, LANES, LANES), F32)],
        compiler_params=_params(("parallel", "parallel", "arbitrary")),
        name="rwkv_chunk",
    )(r, lw, k, v, na, bb, bonus, ln_g.reshape(1, rww), ln_b.reshape(1, rww), s0, ltri, _head_ones())
    sf = sf.reshape(b, hp, 2, hd, 2, hd)
    wkv = jnp.stack([sf[:, :, 0, :, 0, :], sf[:, :, 1, :, 1, :]], axis=2).reshape(b, 2 * hp, hd, hd)
    return o[:, :t], wkv


def _suffix_matrix(tk):
    i = jnp.arange(tk)
    upper = (i[:, None] > i[None, :]).astype(BF16)
    return jnp.concatenate([upper, jnp.ones((tk, tk), BF16)], axis=1)


def _sb_prompt_kernel(bias_ref, q_ref, k_ref, v_ref, mxo_ref, o_ref, acc, carry, *, scale, hd, nh):
    qi = pl.program_id(2)
    tq = q_ref.shape[1]
    mxo = mxo_ref[...]

    def heads_of(ref, rows):
        return jnp.stack([ref[0, rows, i * hd:(i + 1) * hd] for i in range(nh)], axis=0)

    q = (heads_of(q_ref, slice(None)) * scale).astype(BF16)
    m = nh * tq

    def scores(rows):
        z = _dg(q, heads_of(k_ref, rows).astype(BF16), nt=True).reshape(m, rows.size)
        bias = jnp.concatenate([jnp.broadcast_to(bias_ref[i, 0:1, :rows.size], (tq, rows.size))
                                for i in range(nh)], axis=0)
        return _softplus_pair(z + bias)

    def weighted_values(att, rows):
        pv = _dg(att.astype(BF16).reshape(nh, tq, rows.size), heads_of(v_ref, rows).astype(BF16))
        return pv.reshape(m, hd)

    def rows_at(kb, n):
        return pl.ds(pl.multiple_of(kb * tq, tq), n)

    row = lax.broadcasted_iota(jnp.int32, (m, tq), 0) % tq
    col = lax.broadcasted_iota(jnp.int32, (m, tq), 1)
    mask = col < row
    rows = rows_at(qi, tq)
    sp, la = scores(rows)
    lsm = jnp.where(mask, -sp, 0.0)
    suf = _dg(lsm.astype(BF16), mxo)
    att = jnp.where(mask, jnp.exp(la + suf), 0.0)
    acc[...] = weighted_values(att, rows)
    carry[...] = jnp.broadcast_to(suf[:, 0:1] + lsm[:, 0:1], (m, hd))

    def wide(kb, c):
        rows = rows_at(kb - 1, 2 * tq)
        sp, la = scores(rows)
        lsm = -sp
        lsb = lsm.astype(BF16)
        suf = _dg(jnp.concatenate([lsb[:, :tq], lsb[:, tq:]], axis=0), mxo)
        suf_old, suf_new = suf[:m], suf[m:]
        tot_new = suf_new[:, 0:1] + lsm[:, tq:tq + 1]
        tot_old = suf_old[:, 0:1] + lsm[:, 0:1]
        att_old = jnp.exp(la[:, :tq] + suf_old + tot_new)
        att_new = jnp.exp(la[:, tq:] + suf_new)
        pv = weighted_values(jnp.concatenate([att_old, att_new], axis=1), rows)
        cr = carry[...]
        acc[...] += jnp.exp(cr) * pv
        carry[...] = cr + jnp.broadcast_to(tot_old + tot_new, (m, hd))
        return c

    lax.fori_loop(0, qi // 2, lambda i, c: wide(qi - 1 - 2 * i, c), 0)

    @pl.when(qi % 2 == 1)
    def _():
        rows = rows_at(0, tq)
        sp, la = scores(rows)
        att = jnp.exp(la + _dg((-sp).astype(BF16), mxo))
        acc[...] += jnp.exp(carry[...]) * weighted_values(att, rows)

    for i in range(nh):
        o_ref[0, :, i * hd:(i + 1) * hd] = acc[i * tq:(i + 1) * tq, :].astype(o_ref.dtype)


def _sb_prompt(proj, bias, q_off, k_off, v_off, heads, hd):
    b, t, _ = proj.shape
    tq = _tile(t, 256, LANES)
    nh = _tile(heads, SB_HEADS_PER_STEP, 1)
    w = nh * hd
    assert q_off % w == 0 and k_off % w == 0 and v_off % w == 0
    qb, kb, vb = q_off // w, k_off // w, v_off // w
    bias_rows = jnp.broadcast_to(bias.astype(F32)[:, None, None], (heads, 8, 2 * tq))
    return pl.pallas_call(
        functools.partial(_sb_prompt_kernel, scale=hd ** -0.5, hd=hd, nh=nh),
        grid=(b, heads // nh, t // tq),
        in_specs=[pl.BlockSpec((nh, 8, 2 * tq), lambda i, h, j: (h, 0, 0)),
                  pl.BlockSpec((1, tq, w), lambda i, h, j: (i, j, qb + h)),
                  pl.BlockSpec((1, t, w), lambda i, h, j: (i, 0, kb + h)),
                  pl.BlockSpec((1, t, w), lambda i, h, j: (i, 0, vb + h)),
                  pl.BlockSpec((tq, tq), lambda i, h, j: (0, 0))],
        out_specs=pl.BlockSpec((1, tq, w), lambda i, h, j: (i, j, h)),
        out_shape=jax.ShapeDtypeStruct((b, t, heads * hd), BF16),
        scratch_shapes=[pltpu.VMEM((nh * tq, hd), F32), pltpu.VMEM((nh * tq, hd), F32)],
        compiler_params=_params(("parallel", "parallel", "arbitrary")),
        name="sb_prompt",
    )(bias_rows, proj, proj, proj, _suffix_matrix(tq)[:, :tq])


SB_HEADS_PER_STEP = 2
SB_QPAD = 16
SB_PAGES_PER_STEP = 4


def _sb_sample_kernel(pt_ref, q_ref, *refs, heads, hd, tq, scale, pps, hg):
    ng = heads // hg
    group = lambda rs: [rs[i * ng:(i + 1) * ng] for i in range(len(rs) // ng)]
    kn_refs, vn_refs = group(refs[:ng]), group(refs[ng:2 * ng])
    refs = refs[2 * ng:]
    kp_refs, vp_refs = group(refs[:pps * ng]), group(refs[pps * ng:2 * pps * ng])
    bias_ref, mxo_ref, o_ref, zbuf, abuf, pvbuf, acc, carry = refs[2 * pps * ng:]
    j = pl.program_id(1)
    nj = pl.num_programs(1)
    qp = SB_QPAD
    rows = heads * qp

    def pages(k_refs, v_refs, masked):
        n = len(k_refs)
        tk = k_refs[0][0].shape[1]

        def head_rows(page_refs, h):
            ref = page_refs[h // hg].reshape(tk * hg, hd)
            return ref[pl.ds(h % hg, tk, stride=hg), :].astype(BF16)

        qs = []
        for h in range(heads):
            qh = jnp.concatenate([q_ref[0, :, h * hd:(h + 1) * hd] * scale,
                                  jnp.zeros((qp - tq, hd), F32)], axis=0)
            qs.append(qh.astype(BF16))
        for s in range(n):
            for h in range(heads):
                zbuf[s, h * qp:(h + 1) * qp, :] = _dg(qs[h], head_rows(k_refs[s], h), nt=True)
        sp, la = _softplus_pair(zbuf[0:n] + bias_ref[...])
        if masked:
            qidx = lax.broadcasted_iota(jnp.int32, (rows, tk), 0) % qp
            kidx = lax.broadcasted_iota(jnp.int32, (rows, tk), 1)
            mask = kidx < qidx
            lsm = jnp.where(mask, -sp, 0.0)
        else:
            lsm = -sp
        st = _dg(lsm.astype(BF16).reshape(n * rows, tk), mxo_ref[...]).reshape(n, rows, 2 * tk)
        att = jnp.exp(la + st[:, :, :tk])
        if masked:
            att = jnp.where(mask, att, 0.0)
        abuf[0:n] = att
        for s in range(n):
            for h in range(heads):
                ah = abuf[s, h * qp:(h + 1) * qp, :].astype(BF16)
                pvbuf[s, h * qp:(h + 1) * qp, :] = _dg(ah, head_rows(v_refs[s], h))
        return st[:, :, tk:tk + hd]

    @pl.when(j == 0)
    def _():
        carry[...] = pages(kn_refs, vn_refs, True)[0]
        acc[...] = pvbuf[0]

    @pl.when(j > 0)
    def _():
        tots = pages(kp_refs, vp_refs, False)
        cr = carry[...]
        out = acc[...]
        for s in range(pps):
            out = out + jnp.exp(cr) * pvbuf[s]
            cr = cr + tots[s]
        acc[...] = out
        carry[...] = cr

    @pl.when(j == nj - 1)
    def _():
        for h in range(heads):
            o_ref[0, :, h * hd:(h + 1) * hd] = acc[h * qp:h * qp + tq, :]


def _sb_sample(q, k_new, v_new, cache_k, cache_v, layer, page_table, bias):
    nb, tq, sbw = q.shape
    depth, nphys, page, heads, hd = cache_k.shape
    n_pages = page_table.shape[1]
    assert tq <= SB_QPAD and tq <= page and page % LANES == 0
    hg = 8 if heads % 8 == 0 else heads
    ng = heads // hg
    pad = lambda x: jnp.pad(x.reshape(nb, tq, heads, hd), ((0, 0), (0, page - tq), (0, 0), (0, 0)))
    ck = cache_k.reshape(depth * nphys, page, heads, hd)
    cv = cache_v.reshape(depth * nphys, page, heads, hd)
    page_table = page_table + layer * nphys
    rows = heads * SB_QPAD
    bias_rows = jnp.broadcast_to(jnp.repeat(bias.astype(F32), SB_QPAD)[:, None], (rows, page))
    pps = _tile(n_pages, SB_PAGES_PER_STEP, 1)

    def page_map(s, g):
        def index(i, j, pt):
            return (pt[i * n_pages + n_pages - 1 - (jnp.maximum(j, 1) - 1) * pps - s], 0, g, 0)
        return index

    const = lambda i, j, pt: (0, 0)
    blk = (1, page, hg, hd)
    new = [pl.BlockSpec(blk, lambda i, j, pt, g=g: (i, 0, g, 0)) for g in range(ng)]
    pages = [pl.BlockSpec(blk, page_map(s, g)) for s in range(pps) for g in range(ng)]
    return pl.pallas_call(
        functools.partial(_sb_sample_kernel, heads=heads, hd=hd, tq=tq, scale=hd ** -0.5, pps=pps, hg=hg),
        grid_spec=pltpu.PrefetchScalarGridSpec(
            num_scalar_prefetch=1,
            grid=(nb, n_pages // pps + 1),
            in_specs=[pl.BlockSpec((1, tq, sbw), lambda i, j, pt: (i, 0, 0))] + new + new + pages + pages +
                     [pl.BlockSpec((rows, page), const),
                      pl.BlockSpec((page, 2 * page), const)],
            out_specs=pl.BlockSpec((1, tq, sbw), lambda i, j, pt: (i, 0, 0)),
            scratch_shapes=[pltpu.VMEM((pps, rows, page), F32), pltpu.VMEM((pps, rows, page), F32),
                            pltpu.VMEM((pps, rows, hd), F32),
                            pltpu.VMEM((rows, hd), F32), pltpu.VMEM((rows, hd), F32)]),
        out_shape=jax.ShapeDtypeStruct((nb, tq, sbw), F32),
        compiler_params=_params(("parallel", "arbitrary")),
        name="sb_sample",
    )(page_table.reshape(-1).astype(jnp.int32), q, *([pad(k_new)] * ng), *([pad(v_new)] * ng),
      *([ck] * (pps * ng)), *([cv] * (pps * ng)), bias_rows, _suffix_matrix(page))


def _router_kernel(x_ref, g_ref, rw_ref, rb_ref, h_ref, id_ref, wt_ref, *, n_groups, per_group):
    x = x_ref[...]
    ms = jnp.mean(x * x, axis=-1, keepdims=True)
    h = x * lax.rsqrt(ms + RMS_EPS) * g_ref[...]
    h_ref[...] = h
    logits = _dot3(h, rw_ref[...]) + rb_ref[...]
    tm = logits.shape[0]
    lane = lax.broadcasted_iota(jnp.int32, (tm, LANES), 1)
    lane_f = lane.astype(F32)
    neg = jnp.finfo(F32).min
    far = float(LANES)
    is_g = lane < n_groups
    gl = jnp.where(is_g, logits, neg)
    gmax = jnp.max(gl, axis=1, keepdims=True)
    gidx = jnp.min(jnp.where(gl == gmax, lane_f, far), axis=1, keepdims=True).astype(jnp.int32)
    gprob = 1.0 / jnp.sum(jnp.where(is_g, jnp.exp(logits - gmax), 0.0), axis=1, keepdims=True)
    lo = n_groups + gidx * per_group
    el = jnp.where((lane >= lo) & (lane < lo + per_group), logits, neg)
    v1 = jnp.max(el, axis=1, keepdims=True)
    i1 = jnp.min(jnp.where(el == v1, lane_f, far), axis=1, keepdims=True).astype(jnp.int32)
    el2 = jnp.where(lane == i1, neg, el)
    v2 = jnp.max(el2, axis=1, keepdims=True)
    i2 = jnp.min(jnp.where((el2 == v2) & (lane != i1), lane_f, far), axis=1, keepdims=True).astype(jnp.int32)
    t = jnp.exp(v2 - v1)
    w1 = gprob / (1.0 + t)
    w2 = gprob * t / (1.0 + t)
    id_ref[...] = jnp.where(lane == 0, i1 - n_groups, jnp.where(lane == 1, i2 - n_groups, 0))
    wt_ref[...] = jnp.where(lane == 0, w1, jnp.where(lane == 1, w2, 0.0))


def _router(x, g, rg_w, rg_b, re_w, re_b):
    m, d = x.shape
    n_groups = rg_w.shape[1]
    n_exp = re_w.shape[1]
    assert n_groups + n_exp <= LANES
    padc = LANES - n_groups - n_exp
    rw = jnp.concatenate([rg_w, re_w, jnp.zeros((d, padc), F32)], axis=1).astype(F32)
    rb = jnp.concatenate([rg_b, re_b, jnp.zeros((padc,), F32)]).reshape(1, LANES).astype(F32)
    tm = _tile(m, 256)
    row = pl.BlockSpec((tm, d), lambda i: (i, 0))
    nar = pl.BlockSpec((tm, LANES), lambda i: (i, 0))
    h, ids, wts = pl.pallas_call(
        functools.partial(_router_kernel, n_groups=n_groups, per_group=n_exp // n_groups),
        grid=(m // tm,),
        in_specs=[row, pl.BlockSpec((1, d), lambda i: (0, 0)),
                  pl.BlockSpec((d, LANES), lambda i: (0, 0)),
                  pl.BlockSpec((1, LANES), lambda i: (0, 0))],
        out_specs=[row, nar, nar],
        out_shape=[jax.ShapeDtypeStruct((m, d), F32),
                   jax.ShapeDtypeStruct((m, LANES), jnp.int32),
                   jax.ShapeDtypeStruct((m, LANES), F32)],
        compiler_params=_params(("parallel",)),
        name="router",
    )(x, g.reshape(1, d), rw, rb)
    return h, ids[:, :TOP_K], wts


def _moe_kernel(ue_ref, us_ref, un_ref, tok_ref, h_hbm, wg_ref, wu_ref, wd_ref, y_ref,
                xf, xb, wgb, wub, wdb, sem):
    u = pl.program_id(0)
    f = pl.program_id(1)
    n = un_ref[u]

    def row_copy(tok, i):
        return pltpu.make_async_copy(h_hbm.at[pl.ds(tok, 1), :], xf.at[pl.ds(i, 1), :], sem)

    @pl.when((f == 0) & (n > 0))
    def _():
        xf[...] = jnp.zeros_like(xf)
        base = us_ref[u]

        def start(i, c):
            row_copy(tok_ref[base + i], i).start()
            return c

        lax.fori_loop(0, n, start, 0)

    @pl.when((f == 0) & (n == 0))
    def _():
        y_ref[...] = jnp.zeros_like(y_ref)

    @pl.when(n > 0)
    def _():
        wgb[...] = wg_ref[0].astype(BF16)
        wub[...] = wu_ref[0].astype(BF16)
        wdb[...] = wd_ref[0].astype(BF16)

    @pl.when((f == 0) & (n > 0))
    def _():
        def wait(i, c):
            row_copy(0, i).wait()
            return c

        lax.fori_loop(0, n, wait, 0)
        xb[...] = xf[...].astype(BF16)

    @pl.when(n > 0)
    def _():
        n_sub = (n + (MOE_SUB - 1)) // MOE_SUB
        for k in range(1, MOE_UNIT // MOE_SUB + 1):
            m_rows = k * MOE_SUB

            @pl.when(n_sub == k)
            def _():
                x = xb[0:m_rows, :]
                g = _dg(x, wgb[...])
                up = _dg(x, wub[...])
                hid = (g * _sigmoid(g) * up).astype(BF16)
                d = y_ref.shape[1]
                tn = _tile(d, MOE_DOWN_COLS, LANES)

                @pl.when(f == 0)
                def _():
                    for c0 in range(0, d, tn):
                        y_ref[0:m_rows, c0:c0 + tn] = _dg(hid, wdb[:, c0:c0 + tn])
                    if m_rows < MOE_UNIT:
                        y_ref[m_rows:MOE_UNIT, :] = jnp.zeros((MOE_UNIT - m_rows, d), F32)

                @pl.when(f > 0)
                def _():
                    for c0 in range(0, d, tn):
                        y_ref[0:m_rows, c0:c0 + tn] += _dg(hid, wdb[:, c0:c0 + tn])


def _moe_experts(h, tok_sorted, unit_e, unit_start, unit_n, w_gate, w_up, w_down):
    m, d = h.shape
    n_units = unit_e.shape[0]
    de = w_gate.shape[2]
    tf = _tile(de, 256, LANES)
    nf = de // tf

    def fidx(u, f, un):
        return jnp.where(un[u] > 0, f, nf - 1)

    return pl.pallas_call(
        _moe_kernel,
        grid_spec=pltpu.PrefetchScalarGridSpec(
            num_scalar_prefetch=4,
            grid=(n_units, nf),
            in_specs=[pl.BlockSpec(memory_space=pl.ANY),
                      pl.BlockSpec((1, d, tf), lambda u, f, ue, us, un, tok: (ue[u], 0, fidx(u, f, un))),
                      pl.BlockSpec((1, d, tf), lambda u, f, ue, us, un, tok: (ue[u], 0, fidx(u, f, un))),
                      pl.BlockSpec((1, tf, d), lambda u, f, ue, us, un, tok: (ue[u], fidx(u, f, un), 0))],
            out_specs=pl.BlockSpec((MOE_UNIT, d), lambda u, f, ue, us, un, tok: (u, 0)),
            scratch_shapes=[pltpu.VMEM((MOE_UNIT, d), F32), pltpu.VMEM((MOE_UNIT, d), BF16),
                            pltpu.VMEM((d, tf), BF16), pltpu.VMEM((d, tf), BF16),
                            pltpu.VMEM((tf, d), BF16), pltpu.SemaphoreType.DMA(())]),
        out_shape=jax.ShapeDtypeStruct((n_units * MOE_UNIT, d), F32),
        compiler_params=_params(("arbitrary", "arbitrary")),
        name="moe_experts",
    )(unit_e, unit_start, unit_n, tok_sorted, h, w_gate, w_up, w_down)


def _combine_kernel(pos_ref, ys_hbm, w_ref, x_ref, g_ref, o_ref, buf, sems, *, normalize):
    i = pl.program_id(0)
    tc = x_ref.shape[0]
    half = tc // 2
    base = i * tc * TOP_K

    def row_copy(p, s, t, part):
        return pltpu.make_async_copy(ys_hbm.at[pl.ds(p, 1), :], buf.at[s, pl.ds(t, 1), :], sems.at[part])

    def start(part):
        def body(t, c):
            for s in range(TOP_K):
                row_copy(pos_ref[base + TOP_K * t + s], s, t, part).start()
            return c
        lax.fori_loop(part * half, (part + 1) * half, body, 0)

    def wait(part):
        def body(t, c):
            for s in range(TOP_K):
                row_copy(0, s, t, part).wait()
            return c
        lax.fori_loop(part * half, (part + 1) * half, body, 0)

    def finish(part):
        rows = slice(part * half, (part + 1) * half)
        w = w_ref[rows, :]
        x = x_ref[rows, :] + (w[:, 0:1] * buf[0, rows, :] + w[:, 1:2] * buf[1, rows, :])
        if normalize:
            ms = jnp.mean(x * x, axis=-1, keepdims=True)
            x = x * lax.rsqrt(ms + RMS_EPS) * g_ref[...]
        o_ref[rows, :] = x

    start(0)
    start(1)
    wait(0)
    finish(0)
    wait(1)
    finish(1)


def _moe_combine(ys, pos, wts, x, g, normalize):
    m, d = x.shape
    tc = _tile(m, 128, 16)
    return pl.pallas_call(
        functools.partial(_combine_kernel, normalize=normalize),
        grid_spec=pltpu.PrefetchScalarGridSpec(
            num_scalar_prefetch=1,
            grid=(m // tc,),
            in_specs=[pl.BlockSpec(memory_space=pl.ANY),
                      pl.BlockSpec((tc, LANES), lambda i, p: (i, 0)),
                      pl.BlockSpec((tc, d), lambda i, p: (i, 0)),
                      pl.BlockSpec((1, d), lambda i, p: (0, 0))],
            out_specs=pl.BlockSpec((tc, d), lambda i, p: (i, 0)),
            scratch_shapes=[pltpu.VMEM((TOP_K, tc, d), F32), pltpu.SemaphoreType.DMA((2,))]),
        out_shape=jax.ShapeDtypeStruct((m, d), F32),
        compiler_params=_params(("arbitrary",)),
        name="moe_combine",
    )(pos, ys, wts, x, g.reshape(1, d))


def _moe_layer(x, norm_g, rg_w, rg_b, re_w, re_b, w_gate, w_up, w_down, final_g, normalize):
    m, d = x.shape
    n_exp = re_w.shape[1]
    h, expert, wts = _router(x, norm_g, rg_w, rg_b, re_w, re_b)

    n_rows = m * TOP_K
    flat_e = expert.reshape(-1)
    order = jnp.argsort(flat_e).astype(jnp.int32)
    e_sorted = flat_e[order]
    tok_sorted = order // TOP_K
    counts = jnp.zeros((n_exp,), jnp.int32).at[flat_e].add(1)
    start = jnp.cumsum(counts) - counts
    units_e = (counts + MOE_UNIT - 1) // MOE_UNIT
    ucum = jnp.cumsum(units_e)
    ufirst = ucum - units_e
    n_units = n_exp + -(-n_rows // MOE_UNIT)
    uid = jnp.arange(n_units, dtype=jnp.int32)
    total = ucum[-1]
    ue = jnp.minimum(jnp.searchsorted(ucum, uid, side="right"), n_exp - 1).astype(jnp.int32)
    local = uid - ufirst[ue]
    active = uid < total
    un = jnp.where(active, jnp.clip(counts[ue] - local * MOE_UNIT, 0, MOE_UNIT), 0).astype(jnp.int32)
    us = jnp.where(active, start[ue] + local * MOE_UNIT, 0).astype(jnp.int32)
    last_e = ue[jnp.maximum(total - 1, 0)]
    ue = jnp.where(active, ue, last_e).astype(jnp.int32)
    local_row = jnp.arange(n_rows, dtype=jnp.int32) - start[e_sorted]
    pos_sorted = (ufirst[e_sorted] + local_row // MOE_UNIT) * MOE_UNIT + local_row % MOE_UNIT
    pos = jnp.zeros((n_rows,), jnp.int32).at[order].set(pos_sorted.astype(jnp.int32))

    ys = _moe_experts(h, tok_sorted, ue, us, un, w_gate, w_up, w_down)
    return _moe_combine(ys, pos, wts, x, final_g, normalize)


def _mixer(x, shift0, wkv0, past, lw):
    b, t, d = x.shape
    rw_proj = lw["mu"].shape[0]
    rww = lw["w_up_a"].shape[0]
    sbw = lw["w_up_b"].shape[0]
    heads = lw["sb_bias"].shape[0]
    hd = sbw // heads
    q_off, k_off, v_off = rw_proj, rw_proj + sbw, rw_proj + 2 * sbw
    ga_off, gb_off = rw_proj + 3 * sbw, rw_proj + 3 * sbw + d
    x2 = x.reshape(b * t, d)

    h = _rmsnorm(x2, lw["norm1_g"], BF16)
    proj = _matmul(h, lw["w_in"], F32, 1024, 768)
    proj3 = proj.reshape(b, t, -1)

    r, lgw, k2, v, na, bb, bonus = _rwkv_prep(proj3, shift0, lw["mu"], lw["w0"], lw["w2"], lw["a0"],
                                              lw["a2"], lw["kk"], lw["ka"], lw["rk"])
    o_a, wkv_new = _rwkv_recurrence(r, lgw, k2, v, na, bb, bonus, wkv0, lw["ln_g"], lw["ln_b"])
    shift_new = proj3[:, -1, :rw_proj]

    k_new = proj3[:, :, k_off:k_off + sbw]
    v_new = proj3[:, :, v_off:v_off + sbw]
    if past is None:
        o_b = _sb_prompt(proj3, lw["sb_bias"], q_off, k_off, v_off, heads, hd)
    else:
        cache_k, cache_v, layer, page_table = past
        q = proj3[:, :, q_off:q_off + sbw]
        o_b = _sb_sample(q, k_new, v_new, cache_k, cache_v, layer, page_table, lw["sb_bias"]).astype(BF16)

    mixed = _merge_up(o_a.reshape(b * t, rww), o_b.reshape(b * t, sbw), proj, lw["w_up_a"], lw["w_up_b"],
                      ga_off, gb_off)
    x1 = _out_proj(mixed, lw["w_out"], x2)
    return (x1, k_new.reshape(b, t, heads, hd), v_new.reshape(b, t, heads, hd), wkv_new, shift_new)


def kernel(x_prompt, x_sample, cache_k, cache_v, state_wkv, state_shift, page_table, norm1_g, w_in, rwkv_mu, rwkv_w0, rwkv_w2, rwkv_a0, rwkv_a2, rwkv_kk, rwkv_ka, rwkv_rk, rwkv_ln_g, rwkv_ln_b, sb_bias, w_up_a, w_up_b, w_out, norm2_g, router_group_w, router_group_b, router_expert_w, router_expert_b, expert_w_gate, expert_w_up, expert_w_down, normf_g):
    depth = w_in.shape[0]
    bp, tp, d = x_prompt.shape
    bs, ts, _ = x_sample.shape
    rw_proj = rwkv_mu.shape[1]
    rww = w_up_a.shape[1]
    rw_heads = rww // RW_HEAD_DIM
    xp, xs = x_prompt, x_sample
    outs = [[] for _ in range(8)]
    for l in range(depth):
        lw = dict(norm1_g=norm1_g[l], w_in=w_in[l].astype(BF16), mu=rwkv_mu[l], w0=rwkv_w0[l],
                  w2=rwkv_w2[l], a0=rwkv_a0[l], a2=rwkv_a2[l], kk=rwkv_kk[l], ka=rwkv_ka[l],
                  rk=rwkv_rk[l].reshape(-1), ln_g=rwkv_ln_g[l], ln_b=rwkv_ln_b[l], sb_bias=sb_bias[l],
                  w_up_a=w_up_a[l].astype(BF16), w_up_b=w_up_b[l].astype(BF16),
                  w_out=w_out[l].astype(BF16))
        shift0 = jnp.zeros((bp, rw_proj), F32)
        wkv0 = jnp.zeros((bp, rw_heads, RW_HEAD_DIM, RW_HEAD_DIM), F32)
        res_p = _mixer(xp, shift0, wkv0, None, lw)
        res_s = _mixer(xs, state_shift[l], state_wkv[l], (cache_k, cache_v, l, page_table), lw)
        for i in range(4):
            outs[i].append(res_p[1 + i])
            outs[4 + i].append(res_s[1 + i])
        x1 = jnp.concatenate([res_p[0], res_s[0]], axis=0)
        last = l == depth - 1
        x2 = _moe_layer(x1, norm2_g[l], router_group_w[l], router_group_b[l], router_expert_w[l],
                        router_expert_b[l], expert_w_gate[l], expert_w_up[l], expert_w_down[l],
                        normf_g, last)
        xp = x2[:bp * tp].reshape(bp, tp, d)
        xs = x2[bp * tp:].reshape(bs, ts, d)
    return (xp, xs) + tuple(jnp.stack(o) for o in outs)
```

```python
import functools

import jax
import jax.numpy as jnp
from jax import lax
from jax.experimental import pallas as pl
from jax.experimental.pallas import tpu as pltpu

F32 = jnp.float32
BF16 = jnp.bfloat16

LANES = 128
RW_HEAD_DIM = 64
RW_CHUNK = 64
RW_INV_BLOCK = 16
RW_PAIRS_PER_STEP = 8
RW_GN_EPS = 64e-5
RMS_EPS = 1e-6
TOP_K = 2
MOE_SUB = 128
MOE_UNIT = 3 * MOE_SUB
MOE_DOWN_COLS = 512
VMEM_LIMIT_MB = 56


def _params(sem, vmem_mb=VMEM_LIMIT_MB):
    return pltpu.CompilerParams(dimension_semantics=sem, vmem_limit_bytes=vmem_mb << 20)


def _tile(n, pref, mult=8):
    t = min(pref, n)
    while t >= mult:
        if n % t == 0 and t % mult == 0:
            return t
        t -= 1
    return n


def _dg(a, b, nt=False):
    if a.ndim == 3:
        dims = (((2,), (2 if nt else 1,)), ((0,), (0,)))
    else:
        dims = (((1,), (1 if nt else 0,)), ((), ()))
    return lax.dot_general(a, b, dims, preferred_element_type=F32)


def _split(x):
    hi = x.astype(BF16)
    lo = (x - hi.astype(F32)).astype(BF16)
    return hi, lo


def _dot1(a, b, nt=False):
    return _dg(a.astype(BF16), b.astype(BF16), nt)


def _dot3(a, b, nt=False):
    ah, al = _split(a)
    bh, bl = _split(b)
    return _dg(ah, bh, nt) + (_dg(ah, bl, nt) + _dg(al, bh, nt))


def _dot2x(a, b_exact):
    ah, al = _split(a)
    return _dg(ah, b_exact) + _dg(al, b_exact)


def _sigmoid(x):
    return 1.0 / (1.0 + jnp.exp(-x))


def _softplus(x):
    return jnp.maximum(x, 0.0) + jnp.log1p(jnp.exp(-jnp.abs(x)))


def _softplus_pair(z):
    l = jnp.log(1.0 + jnp.exp(-jnp.abs(z)))
    return jnp.maximum(z, 0.0) + l, jnp.minimum(z, 0.0) - l


def _rmsnorm_kernel(x_ref, g_ref, o_ref):
    x = x_ref[...]
    ms = jnp.mean(x * x, axis=-1, keepdims=True)
    o_ref[...] = (x * lax.rsqrt(ms + RMS_EPS) * g_ref[...]).astype(o_ref.dtype)


def _rmsnorm(x, g, out_dtype):
    m, d = x.shape
    tm = _tile(m, 256)
    return pl.pallas_call(
        _rmsnorm_kernel,
        grid=(m // tm,),
        in_specs=[pl.BlockSpec((tm, d), lambda i: (i, 0)),
                  pl.BlockSpec((1, d), lambda i: (0, 0))],
        out_specs=pl.BlockSpec((tm, d), lambda i: (i, 0)),
        out_shape=jax.ShapeDtypeStruct((m, d), out_dtype),
        compiler_params=_params(("parallel",)),
        name="rmsnorm",
    )(x, g.reshape(1, d))


def _mm_kernel(a_ref, b_ref, o_ref):
    o_ref[...] = jnp.dot(a_ref[...], b_ref[...], preferred_element_type=F32).astype(o_ref.dtype)


def _matmul(a, b, out_dtype, tm_pref, tn_pref):
    m, k = a.shape
    n = b.shape[1]
    tm = _tile(m, tm_pref, 16)
    tn = _tile(n, tn_pref, LANES)
    return pl.pallas_call(
        _mm_kernel,
        grid=(m // tm, n // tn),
        in_specs=[pl.BlockSpec((tm, k), lambda i, j: (i, 0)),
                  pl.BlockSpec((k, tn), lambda i, j: (0, j))],
        out_specs=pl.BlockSpec((tm, tn), lambda i, j: (i, j)),
        out_shape=jax.ShapeDtypeStruct((m, n), out_dtype),
        compiler_params=_params(("parallel", "arbitrary")),
        name="in_proj",
    )(a, b)


def _merge_kernel(oa_ref, ob_ref, wa_ref, wb_ref, ga_ref, gb_ref, o_ref):
    ya = jnp.dot(oa_ref[...], wa_ref[...], preferred_element_type=F32)
    yb = jnp.dot(ob_ref[...], wb_ref[...], preferred_element_type=F32)
    o_ref[...] = (_sigmoid(ga_ref[...]) * ya + _sigmoid(gb_ref[...]) * yb).astype(o_ref.dtype)


def _merge_up(o_a, o_b, proj, w_up_a, w_up_b, ga_off, gb_off):
    m, ka = o_a.shape
    kb = o_b.shape[1]
    d = w_up_a.shape[1]
    tm = _tile(m, 1024, 16)
    tn = 256
    assert d % tn == 0 and ga_off % tn == 0 and gb_off % tn == 0
    ga_blk, gb_blk = ga_off // tn, gb_off // tn
    return pl.pallas_call(
        _merge_kernel,
        grid=(m // tm, d // tn),
        in_specs=[pl.BlockSpec((tm, ka), lambda i, j: (i, 0)),
                  pl.BlockSpec((tm, kb), lambda i, j: (i, 0)),
                  pl.BlockSpec((ka, tn), lambda i, j: (0, j)),
                  pl.BlockSpec((kb, tn), lambda i, j: (0, j)),
                  pl.BlockSpec((tm, tn), lambda i, j: (i, ga_blk + j)),
                  pl.BlockSpec((tm, tn), lambda i, j: (i, gb_blk + j))],
        out_specs=pl.BlockSpec((tm, tn), lambda i, j: (i, j)),
        out_shape=jax.ShapeDtypeStruct((m, d), BF16),
        compiler_params=_params(("parallel", "arbitrary")),
        name="merge_up",
    )(o_a, o_b, w_up_a, w_up_b, proj, proj)


def _outproj_kernel(a_ref, b_ref, x_ref, o_ref):
    o_ref[...] = x_ref[...] + jnp.dot(a_ref[...], b_ref[...], preferred_element_type=F32)


def _out_proj(mixed, w_out, x):
    m, k = mixed.shape
    n = w_out.shape[1]
    tm = _tile(m, 1024, 16)
    tn = _tile(n, 512, LANES)
    return pl.pallas_call(
        _outproj_kernel,
        grid=(m // tm, n // tn),
        in_specs=[pl.BlockSpec((tm, k), lambda i, j: (i, 0)),
                  pl.BlockSpec((k, tn), lambda i, j: (0, j)),
                  pl.BlockSpec((tm, tn), lambda i, j: (i, j))],
        out_specs=pl.BlockSpec((tm, tn), lambda i, j: (i, j)),
        out_shape=jax.ShapeDtypeStruct((m, n), F32),
        compiler_params=_params(("parallel", "arbitrary")),
        name="out_proj",
    )(mixed, w_out, x)


def _rwkv_prep_kernel(p_ref, sh_ref, mu_ref, w0_ref, w2_ref, a0_ref, a2_ref, kkw_ref, kaw_ref,
                      rkw_ref, ones_ref,
                      r_out, lw_out, k_out, v_out, na_out, bb_out, bon_out,
                      carry, *, rww, dl, al):
    t = pl.program_id(1)
    tt = p_ref.shape[1]

    @pl.when(t == 0)
    def _():
        carry[0:1, :] = sh_ref[0]

    row0 = lax.broadcasted_iota(jnp.int32, (tt, 1), 0) == 0

    def mixed(lo, hi):
        p = p_ref[0, :, lo:hi]
        prev = jnp.where(row0, carry[0:1, lo:hi], pltpu.roll(p, 1, 0))
        return p + (prev - p) * mu_ref[:, lo:hi]

    wd = mixed(3 * rww, 3 * rww + dl)
    ad = mixed(3 * rww + dl, 3 * rww + dl + al)
    w_lora = _dot3(jnp.tanh(wd), w2_ref[...])
    a_lora = _dot3(ad, a2_ref[...])
    ones = ones_ref[...]
    for s in range(rww // LANES):
        c0, c1 = s * LANES, (s + 1) * LANES
        r = mixed(c0, c1)
        k = mixed(rww + c0, rww + c1)
        v = mixed(2 * rww + c0, 2 * rww + c1)
        w = w0_ref[:, c0:c1] + w_lora[:, c0:c1]
        lw = -jnp.exp(-_softplus(-w) - 0.5)
        a = _sigmoid(a0_ref[:, c0:c1] + a_lora[:, c0:c1])
        kk = k * kkw_ref[:, c0:c1]
        ss = _dot2x(kk * kk, ones)
        kk = kk / jnp.maximum(jnp.sqrt(ss), 1e-12)
        k2 = k * (1.0 + (a - 1.0) * kaw_ref[:, c0:c1])
        rk = _dot2x(r * k2 * rkw_ref[:, c0:c1], ones)
        r_out[0, :, c0:c1] = r
        lw_out[0, :, c0:c1] = lw
        k_out[0, :, c0:c1] = k2
        v_out[0, :, c0:c1] = v
        na_out[0, :, c0:c1] = -kk
        bb_out[0, :, c0:c1] = kk * a
        bon_out[0, :, c0:c1] = rk * v
    carry[0:1, :] = p_ref[0, tt - 1:tt, :]


def _head_ones():
    i = jnp.arange(LANES) // RW_HEAD_DIM
    return (i[:, None] == i[None, :]).astype(BF16)


def _rwkv_prep(proj, shift0, mu, w0, w2, a0, a2, kkw, kaw, rkw):
    b, t, _ = proj.shape
    p = mu.shape[0]
    rww = w0.shape[0]
    dl, al = w2.shape[0], a2.shape[0]
    tt = _tile(t, 128)
    row = lambda x: x.reshape(1, -1)
    out = jax.ShapeDtypeStruct((b, t, rww), F32)
    vec = lambda n: pl.BlockSpec((1, n), lambda i, j: (0, 0))
    full = lambda s: pl.BlockSpec(s, lambda i, j: (0, 0))
    blk = pl.BlockSpec((1, tt, rww), lambda i, j: (i, j, 0))
    return pl.pallas_call(
        functools.partial(_rwkv_prep_kernel, rww=rww, dl=dl, al=al),
        grid=(b, t // tt),
        in_specs=[pl.BlockSpec((1, tt, p), lambda i, j: (i, j, 0)),
                  pl.BlockSpec((1, 1, p), lambda i, j: (i, 0, 0)),
                  vec(p), vec(rww), full((dl, rww)), vec(rww), full((al, rww)),
                  vec(rww), vec(rww), vec(rww), full((LANES, LANES))],
        out_specs=[blk] * 7,
        out_shape=[out] * 7,
        scratch_shapes=[pltpu.VMEM((8, p), F32)],
        compiler_params=_params(("parallel", "arbitrary")),
        name="rwkv_prep",
    )(proj, shift0.reshape(b, 1, p), row(mu), row(w0), w2, row(a0), a2, row(kkw), row(kaw),
      row(rkw), _head_ones())


def _tri_inv(a, n, nb):
    row = lax.broadcasted_iota(jnp.int32, (n, n), 0)
    col = lax.broadcasted_iota(jnp.int32, (n, n), 1)
    dmask = (row // RW_INV_BLOCK) == (col // RW_INV_BLOCK)
    eye = jnp.where(row == col, 1.0, 0.0)
    ad = jnp.where(dmask, a, 0.0)
    ao = jnp.where(dmask, 0.0, a)
    dinv = eye + ad
    p = ad
    k = 1
    while 2 * k < RW_INV_BLOCK:
        p = _dot1(p, p)
        dinv = dinv + _dot1(dinv, p)
        k *= 2
    bm = _dot1(dinv, ao)
    x = eye + bm
    p = bm
    k = 1
    while 2 * k < nb:
        p = _dot1(p, p)
        x = x + _dot1(x, p)
        k *= 2
    return _dot1(x, dinv)


def _rwkv_chunk_math(r, cum, lw, kk, v, na, bb, s):
    ch = r.shape[1]
    n = 2 * ch
    cl = cum[:, ch - 1:ch, :]
    e_neg = jnp.exp(-cum)
    e_end = jnp.exp(cl - cum)
    a_t = na * jnp.exp(cum - lw)
    r_t = r * jnp.exp(cum)
    lane_lo = lax.broadcasted_iota(jnp.int32, (ch, LANES), 1) < RW_HEAD_DIM

    def hat(x):
        return jnp.concatenate([jnp.where(lane_lo, x, 0.0), jnp.where(lane_lo, 0.0, x)], axis=1)

    ath, rth, vh = hat(a_t), hat(r_t), hat(v)
    btkt = jnp.concatenate([hat(bb * e_neg), hat(kk * e_neg)], axis=1)
    bbh, kbh = hat(bb * e_end), hat(kk * e_end)

    row = lax.broadcasted_iota(jnp.int32, (n, n), 0)
    col = lax.broadcasted_iota(jnp.int32, (n, n), 1)
    strict = row > col
    incl = row >= col
    sc = _dot1(jnp.concatenate([ath, rth], axis=1), btkt, nt=True)
    a_ab = jnp.where(strict, sc[:, :n, :n], 0.0)
    a_ak = jnp.where(strict, sc[:, :n, n:], 0.0)
    a_rb = jnp.where(incl, sc[:, n:, :n], 0.0)
    a_rk = jnp.where(incl, sc[:, n:, n:], 0.0)

    tinv = _tri_inv(a_ab, n, ch // RW_INV_BLOCK)
    x = _dot1(tinv, jnp.concatenate([_dot1(a_ak, vh), ath], axis=2))
    u0, a2 = x[:, :, :LANES], x[:, :, LANES:]
    z = _dot1(a_rb, x)
    y0 = z[:, :, :LANES] + _dot1(a_rk, vh)
    rh = rth + z[:, :, LANES:]
    yh = _dot1(rh, s, nt=True) + y0
    y = yh[:, :ch] + yh[:, ch:]

    tr = lambda t: jnp.swapaxes(t, 1, 2)
    m = jnp.where(row == col, jnp.exp(cl), 0.0) + _dot1(tr(a2), bbh)
    nn = _dot1(jnp.concatenate([tr(u0), tr(vh)], axis=2), jnp.concatenate([bbh, kbh], axis=1))
    return y, _dot3(s, m) + nn


def _rwkv_chunk_kernel(r_ref, lw_ref, k_ref, v_ref, na_ref, bb_ref, bon_ref, lng_ref, lnb_ref,
                       s0_ref, ltri_ref, ones_ref, o_ref, sf_ref, s_scr):
    c = pl.program_id(2)
    pairs = s_scr.shape[0]
    ch = r_ref.shape[1]

    @pl.when(c == 0)
    def _():
        s_scr[...] = s0_ref[0]

    def pairwise(x):
        return jnp.stack([x[:, g * LANES:(g + 1) * LANES] for g in range(pairs)], axis=0)

    lw = lw_ref[0]
    cum = _dot2x_lhs(ltri_ref[...], lw)
    y, s_new = _rwkv_chunk_math(pairwise(r_ref[0]), pairwise(cum), pairwise(lw), pairwise(k_ref[0]),
                                pairwise(v_ref[0]), pairwise(na_ref[0]), pairwise(bb_ref[0]), s_scr[...])
    s_scr[...] = s_new
    sf_ref[0] = s_new

    ones = ones_ref[...]
    y2 = y.reshape(pairs * ch, LANES)
    mean = _dot2x(y2, ones) * (1.0 / RW_HEAD_DIM)
    d = y2 - mean
    var = _dot2x(d * d, ones) * (1.0 / RW_HEAD_DIM)
    yn = d * lax.rsqrt(var + RW_GN_EPS)
    for g in range(pairs):
        sl = slice(g * LANES, (g + 1) * LANES)
        out = yn[g * ch:(g + 1) * ch] * lng_ref[:, sl] + lnb_ref[:, sl] + bon_ref[0, :, sl]
        o_ref[0, :, sl] = out.astype(o_ref.dtype)


def _dot2x_lhs(a_exact, b):
    bh, bl = _split(b)
    return _dg(a_exact, bh) + _dg(a_exact, bl)


def _rwkv_recurrence(r, lw, k, v, na, bb, bonus, wkv0, ln_g, ln_b):
    b, t, rww = r.shape
    hp = rww // LANES
    hd = RW_HEAD_DIM
    ch = RW_CHUNK
    tp = -(-t // ch) * ch
    if tp != t:
        pad = lambda x: jnp.pad(x, ((0, 0), (0, tp - t), (0, 0)))
        r, lw, k, v, na, bb, bonus = map(pad, (r, lw, k, v, na, bb, bonus))
    nc = tp // ch
    w0 = wkv0.astype(F32).reshape(b, hp, 2, hd, hd)
    zero = jnp.zeros((b, hp, hd, hd), F32)
    s0 = jnp.concatenate([jnp.concatenate([w0[:, :, 0], zero], axis=-1),
                          jnp.concatenate([zero, w0[:, :, 1]], axis=-1)], axis=-2)
    ltri = (jnp.arange(ch)[:, None] >= jnp.arange(ch)[None, :]).astype(BF16)
    pairs = _tile(hp, RW_PAIRS_PER_STEP, 1)
    blk = pl.BlockSpec((1, ch, pairs * LANES), lambda i, j, c: (i, c, j))
    vec = pl.BlockSpec((1, pairs * LANES), lambda i, j, c: (0, j))
    st = pl.BlockSpec((1, pairs, LANES, LANES), lambda i, j, c: (i, j, 0, 0))
    full = lambda s: pl.BlockSpec(s, lambda i, j, c: (0, 0))
    o, sf = pl.pallas_call(
        _rwkv_chunk_kernel,
        grid=(b, hp // pairs, nc),
        in_specs=[blk] * 7 + [vec, vec, st, full((ch, ch)), full((LANES, LANES))],
        out_specs=[blk, st],
        out_shape=[jax.ShapeDtypeStruct((b, tp, rww), BF16),
                   jax.ShapeDtypeStruct((b, hp, LANES, LANES), F32)],
        scratch_shapes=[pltpu.VMEM((pairs, LANES, LANES), F32)],
        compiler_params=_params(("parallel", "parallel", "arbitrary")),
        name="rwkv_chunk",
    )(r, lw, k, v, na, bb, bonus, ln_g.reshape(1, rww), ln_b.reshape(1, rww), s0, ltri, _head_ones())
    sf = sf.reshape(b, hp, 2, hd, 2, hd)
    wkv = jnp.stack([sf[:, :, 0, :, 0, :], sf[:, :, 1, :, 1, :]], axis=2).reshape(b, 2 * hp, hd, hd)
    return o[:, :t], wkv


def _suffix_matrix(tk):
    i = jnp.arange(tk)
    upper = (i[:, None] > i[None, :]).astype(BF16)
    return jnp.concatenate([upper, jnp.ones((tk, tk), BF16)], axis=1)


def _sb_prompt_kernel(bias_ref, q_ref, k_ref, v_ref, mxo_ref, o_ref, acc, carry, *, scale, hd, nh):
    qi = pl.program_id(2)
    tq = q_ref.shape[1]
    mxo = mxo_ref[...]

    def heads_of(ref, rows):
        return jnp.stack([ref[0, rows, i * hd:(i + 1) * hd] for i in range(nh)], axis=0)

    q = (heads_of(q_ref, slice(None)) * scale).astype(BF16)
    m = nh * tq

    def scores(rows):
        z = _dg(q, heads_of(k_ref, rows).astype(BF16), nt=True).reshape(m, rows.size)
        bias = jnp.concatenate([jnp.broadcast_to(bias_ref[i, 0:1, :rows.size], (tq, rows.size))
                                for i in range(nh)], axis=0)
        return _softplus_pair(z + bias)

    def weighted_values(att, rows):
        pv = _dg(att.astype(BF16).reshape(nh, tq, rows.size), heads_of(v_ref, rows).astype(BF16))
        return pv.reshape(m, hd)

    def rows_at(kb, n):
        return pl.ds(pl.multiple_of(kb * tq, tq), n)

    row = lax.broadcasted_iota(jnp.int32, (m, tq), 0) % tq
    col = lax.broadcasted_iota(jnp.int32, (m, tq), 1)
    mask = col < row
    rows = rows_at(qi, tq)
    sp, la = scores(rows)
    lsm = jnp.where(mask, -sp, 0.0)
    suf = _dg(lsm.astype(BF16), mxo)
    att = jnp.where(mask, jnp.exp(la + suf), 0.0)
    acc[...] = weighted_values(att, rows)
    carry[...] = jnp.broadcast_to(suf[:, 0:1] + lsm[:, 0:1], (m, hd))

    def wide(kb, c):
        rows = rows_at(kb - 1, 2 * tq)
        sp, la = scores(rows)
        lsm = -sp
        lsb = lsm.astype(BF16)
        suf = _dg(jnp.concatenate([lsb[:, :tq], lsb[:, tq:]], axis=0), mxo)
        suf_old, suf_new = suf[:m], suf[m:]
        tot_new = suf_new[:, 0:1] + lsm[:, tq:tq + 1]
        tot_old = suf_old[:, 0:1] + lsm[:, 0:1]
        att_old = jnp.exp(la[:, :tq] + suf_old + tot_new)
        att_new = jnp.exp(la[:, tq:] + suf_new)
        pv = weighted_values(jnp.concatenate([att_old, att_new], axis=1), rows)
        cr = carry[...]
        acc[...] += jnp.exp(cr) * pv
        carry[...] = cr + jnp.broadcast_to(tot_old + tot_new, (m, hd))
        return c

    lax.fori_loop(0, qi // 2, lambda i, c: wide(qi - 1 - 2 * i, c), 0)

    @pl.when(qi % 2 == 1)
    def _():
        rows = rows_at(0, tq)
        sp, la = scores(rows)
        att = jnp.exp(la + _dg((-sp).astype(BF16), mxo))
        acc[...] += jnp.exp(carry[...]) * weighted_values(att, rows)

    for i in range(nh):
        o_ref[0, :, i * hd:(i + 1) * hd] = acc[i * tq:(i + 1) * tq, :].astype(o_ref.dtype)


def _sb_prompt(proj, bias, q_off, k_off, v_off, heads, hd):
    b, t, _ = proj.shape
    tq = _tile(t, 256, LANES)
    nh = _tile(heads, SB_HEADS_PER_STEP, 1)
    w = nh * hd
    assert q_off % w == 0 and k_off % w == 0 and v_off % w == 0
    qb, kb, vb = q_off // w, k_off // w, v_off // w
    bias_rows = jnp.broadcast_to(bias.astype(F32)[:, None, None], (heads, 8, 2 * tq))
    return pl.pallas_call(
        functools.partial(_sb_prompt_kernel, scale=hd ** -0.5, hd=hd, nh=nh),
        grid=(b, heads // nh, t // tq),
        in_specs=[pl.BlockSpec((nh, 8, 2 * tq), lambda i, h, j: (h, 0, 0)),
                  pl.BlockSpec((1, tq, w), lambda i, h, j: (i, j, qb + h)),
                  pl.BlockSpec((1, t, w), lambda i, h, j: (i, 0, kb + h)),
                  pl.BlockSpec((1, t, w), lambda i, h, j: (i, 0, vb + h)),
                  pl.BlockSpec((tq, tq), lambda i, h, j: (0, 0))],
        out_specs=pl.BlockSpec((1, tq, w), lambda i, h, j: (i, j, h)),
        out_shape=jax.ShapeDtypeStruct((b, t, heads * hd), BF16),
        scratch_shapes=[pltpu.VMEM((nh * tq, hd), F32), pltpu.VMEM((nh * tq, hd), F32)],
        compiler_params=_params(("parallel", "parallel", "arbitrary")),
        name="sb_prompt",
    )(bias_rows, proj, proj, proj, _suffix_matrix(tq)[:, :tq])


SB_HEADS_PER_STEP = 2
SB_QPAD = 16
SB_PAGES_PER_STEP = 8


def _sb_sample_kernel(pt_ref, q_ref, *refs, heads, hd, tq, scale, pps, hg):
    ng = heads // hg
    group = lambda rs: [rs[i * ng:(i + 1) * ng] for i in range(len(rs) // ng)]
    kn_refs, vn_refs = group(refs[:ng]), group(refs[ng:2 * ng])
    refs = refs[2 * ng:]
    kp_refs, vp_refs = group(refs[:pps * ng]), group(refs[pps * ng:2 * pps * ng])
    bias_ref, mxo_ref, o_ref, zbuf, abuf, pvbuf, acc, carry = refs[2 * pps * ng:]
    j = pl.program_id(1)
    nj = pl.num_programs(1)
    qp = SB_QPAD
    rows = heads * qp

    def pages(k_refs, v_refs, masked):
        n = len(k_refs)
        tk = k_refs[0][0].shape[1]

        def head_rows(page_refs, h):
            ref = page_refs[h // hg].reshape(tk * hg, hd)
            return ref[pl.ds(h % hg, tk, stride=hg), :].astype(BF16)

        qs = []
        for h in range(heads):
            qh = jnp.concatenate([q_ref[0, :, h * hd:(h + 1) * hd] * scale,
                                  jnp.zeros((qp - tq, hd), F32)], axis=0)
            qs.append(qh.astype(BF16))
        for s in range(n):
            for h in range(heads):
                zbuf[s, h * qp:(h + 1) * qp, :] = _dg(qs[h], head_rows(k_refs[s], h), nt=True)
        sp, la = _softplus_pair(zbuf[0:n] + bias_ref[...])
        if masked:
            qidx = lax.broadcasted_iota(jnp.int32, (rows, tk), 0) % qp
            kidx = lax.broadcasted_iota(jnp.int32, (rows, tk), 1)
            mask = kidx < qidx
            lsm = jnp.where(mask, -sp, 0.0)
        else:
            lsm = -sp
        st = _dg(lsm.astype(BF16).reshape(n * rows, tk), mxo_ref[...]).reshape(n, rows, 2 * tk)
        att = jnp.exp(la + st[:, :, :tk])
        if masked:
            att = jnp.where(mask, att, 0.0)
        abuf[0:n] = att
        for s in range(n):
            for h in range(heads):
                ah = abuf[s, h * qp:(h + 1) * qp, :].astype(BF16)
                pvbuf[s, h * qp:(h + 1) * qp, :] = _dg(ah, head_rows(v_refs[s], h))
        return st[:, :, tk:tk + hd]

    @pl.when(j == 0)
    def _():
        carry[...] = pages(kn_refs, vn_refs, True)[0]
        acc[...] = pvbuf[0]

    @pl.when(j > 0)
    def _():
        tots = pages(kp_refs, vp_refs, False)
        cr = carry[...]
        out = acc[...]
        for s in range(pps):
            out = out + jnp.exp(cr) * pvbuf[s]
            cr = cr + tots[s]
        acc[...] = out
        carry[...] = cr

    @pl.when(j == nj - 1)
    def _():
        for h in range(heads):
            o_ref[0, :, h * hd:(h + 1) * hd] = acc[h * qp:h * qp + tq, :]


def _sb_sample(q, k_new, v_new, cache_k, cache_v, layer, page_table, bias):
    nb, tq, sbw = q.shape
    depth, nphys, page, heads, hd = cache_k.shape
    n_pages = page_table.shape[1]
    assert tq <= SB_QPAD and tq <= page and page % LANES == 0
    hg = 8 if heads % 8 == 0 else heads
    ng = heads // hg
    pad = lambda x: jnp.pad(x.reshape(nb, tq, heads, hd), ((0, 0), (0, page - tq), (0, 0), (0, 0)))
    ck = cache_k.reshape(depth * nphys, page, heads, hd)
    cv = cache_v.reshape(depth * nphys, page, heads, hd)
    page_table = page_table + layer * nphys
    rows = heads * SB_QPAD
    bias_rows = jnp.broadcast_to(jnp.repeat(bias.astype(F32), SB_QPAD)[:, None], (rows, page))
    pps = _tile(n_pages, SB_PAGES_PER_STEP, 1)

    def page_map(s, g):
        def index(i, j, pt):
            return (pt[i * n_pages + n_pages - 1 - (jnp.maximum(j, 1) - 1) * pps - s], 0, g, 0)
        return index

    const = lambda i, j, pt: (0, 0)
    blk = (1, page, hg, hd)
    new = [pl.BlockSpec(blk, lambda i, j, pt, g=g: (i, 0, g, 0)) for g in range(ng)]
    pages = [pl.BlockSpec(blk, page_map(s, g)) for s in range(pps) for g in range(ng)]
    return pl.pallas_call(
        functools.partial(_sb_sample_kernel, heads=heads, hd=hd, tq=tq, scale=hd ** -0.5, pps=pps, hg=hg),
        grid_spec=pltpu.PrefetchScalarGridSpec(
            num_scalar_prefetch=1,
            grid=(nb, n_pages // pps + 1),
            in_specs=[pl.BlockSpec((1, tq, sbw), lambda i, j, pt: (i, 0, 0))] + new + new + pages + pages +
                     [pl.BlockSpec((rows, page), const),
                      pl.BlockSpec((page, 2 * page), const)],
            out_specs=pl.BlockSpec((1, tq, sbw), lambda i, j, pt: (i, 0, 0)),
            scratch_shapes=[pltpu.VMEM((pps, rows, page), F32), pltpu.VMEM((pps, rows, page), F32),
                            pltpu.VMEM((pps, rows, hd), F32),
                            pltpu.VMEM((rows, hd), F32), pltpu.VMEM((rows, hd), F32)]),
        out_shape=jax.ShapeDtypeStruct((nb, tq, sbw), F32),
        compiler_params=_params(("parallel", "arbitrary")),
        name="sb_sample",
    )(page_table.reshape(-1).astype(jnp.int32), q, *([pad(k_new)] * ng), *([pad(v_new)] * ng),
      *([ck] * (pps * ng)), *([cv] * (pps * ng)), bias_rows, _suffix_matrix(page))


def _router_kernel(x_ref, g_ref, rw_ref, rb_ref, h_ref, id_ref, wt_ref, *, n_groups, per_group):
    x = x_ref[...]
    ms = jnp.mean(x * x, axis=-1, keepdims=True)
    h = x * lax.rsqrt(ms + RMS_EPS) * g_ref[...]
    h_ref[...] = h
    logits = _dot3(h, rw_ref[...]) + rb_ref[...]
    tm = logits.shape[0]
    lane = lax.broadcasted_iota(jnp.int32, (tm, LANES), 1)
    lane_f = lane.astype(F32)
    neg = jnp.finfo(F32).min
    far = float(LANES)
    is_g = lane < n_groups
    gl = jnp.where(is_g, logits, neg)
    gmax = jnp.max(gl, axis=1, keepdims=True)
    gidx = jnp.min(jnp.where(gl == gmax, lane_f, far), axis=1, keepdims=True).astype(jnp.int32)
    gprob = 1.0 / jnp.sum(jnp.where(is_g, jnp.exp(logits - gmax), 0.0), axis=1, keepdims=True)
    lo = n_groups + gidx * per_group
    el = jnp.where((lane >= lo) & (lane < lo + per_group), logits, neg)
    v1 = jnp.max(el, axis=1, keepdims=True)
    i1 = jnp.min(jnp.where(el == v1, lane_f, far), axis=1, keepdims=True).astype(jnp.int32)
    el2 = jnp.where(lane == i1, neg, el)
    v2 = jnp.max(el2, axis=1, keepdims=True)
    i2 = jnp.min(jnp.where((el2 == v2) & (lane != i1), lane_f, far), axis=1, keepdims=True).astype(jnp.int32)
    t = jnp.exp(v2 - v1)
    w1 = gprob / (1.0 + t)
    w2 = gprob * t / (1.0 + t)
    id_ref[...] = jnp.where(lane == 0, i1 - n_groups, jnp.where(lane == 1, i2 - n_groups, 0))
    wt_ref[...] = jnp.where(lane == 0, w1, jnp.where(lane == 1, w2, 0.0))


def _router(x, g, rg_w, rg_b, re_w, re_b):
    m, d = x.shape
    n_groups = rg_w.shape[1]
    n_exp = re_w.shape[1]
    assert n_groups + n_exp <= LANES
    padc = LANES - n_groups - n_exp
    rw = jnp.concatenate([rg_w, re_w, jnp.zeros((d, padc), F32)], axis=1).astype(F32)
    rb = jnp.concatenate([rg_b, re_b, jnp.zeros((padc,), F32)]).reshape(1, LANES).astype(F32)
    tm = _tile(m, 256)
    row = pl.BlockSpec((tm, d), lambda i: (i, 0))
    nar = pl.BlockSpec((tm, LANES), lambda i: (i, 0))
    h, ids, wts = pl.pallas_call(
        functools.partial(_router_kernel, n_groups=n_groups, per_group=n_exp // n_groups),
        grid=(m // tm,),
        in_specs=[row, pl.BlockSpec((1, d), lambda i: (0, 0)),
                  pl.BlockSpec((d, LANES), lambda i: (0, 0)),
                  pl.BlockSpec((1, LANES), lambda i: (0, 0))],
        out_specs=[row, nar, nar],
        out_shape=[jax.ShapeDtypeStruct((m, d), F32),
                   jax.ShapeDtypeStruct((m, LANES), jnp.int32),
                   jax.ShapeDtypeStruct((m, LANES), F32)],
        compiler_params=_params(("parallel",)),
        name="router",
    )(x, g.reshape(1, d), rw, rb)
    return h, ids[:, :TOP_K], wts


def _moe_kernel(ue_ref, us_ref, un_ref, tok_ref, h_hbm, wg_ref, wu_ref, wd_ref, y_ref,
                xf, xb, wgb, wub, wdb, sem):
    u = pl.program_id(0)
    f = pl.program_id(1)
    n = un_ref[u]

    def row_copy(tok, i):
        return pltpu.make_async_copy(h_hbm.at[pl.ds(tok, 1), :], xf.at[pl.ds(i, 1), :], sem)

    @pl.when((f == 0) & (n > 0))
    def _():
        xf[...] = jnp.zeros_like(xf)
        base = us_ref[u]

        def start(i, c):
            row_copy(tok_ref[base + i], i).start()
            return c

        lax.fori_loop(0, n, start, 0)

    @pl.when((f == 0) & (n == 0))
    def _():
        y_ref[...] = jnp.zeros_like(y_ref)

    @pl.when(n > 0)
    def _():
        wgb[...] = wg_ref[0].astype(BF16)
        wub[...] = wu_ref[0].astype(BF16)
        wdb[...] = wd_ref[0].astype(BF16)

    @pl.when((f == 0) & (n > 0))
    def _():
        def wait(i, c):
            row_copy(0, i).wait()
            return c

        lax.fori_loop(0, n, wait, 0)
        xb[...] = xf[...].astype(BF16)

    @pl.when(n > 0)
    def _():
        n_sub = (n + (MOE_SUB - 1)) // MOE_SUB
        for k in range(1, MOE_UNIT // MOE_SUB + 1):
            m_rows = k * MOE_SUB

            @pl.when(n_sub == k)
            def _():
                x = xb[0:m_rows, :]
                g = _dg(x, wgb[...])
                up = _dg(x, wub[...])
                hid = (g * _sigmoid(g) * up).astype(BF16)
                d = y_ref.shape[1]
                tn = _tile(d, MOE_DOWN_COLS, LANES)

                @pl.when(f == 0)
                def _():
                    for c0 in range(0, d, tn):
                        y_ref[0:m_rows, c0:c0 + tn] = _dg(hid, wdb[:, c0:c0 + tn])
                    if m_rows < MOE_UNIT:
                        y_ref[m_rows:MOE_UNIT, :] = jnp.zeros((MOE_UNIT - m_rows, d), F32)

                @pl.when(f > 0)
                def _():
                    for c0 in range(0, d, tn):
                        y_ref[0:m_rows, c0:c0 + tn] += _dg(hid, wdb[:, c0:c0 + tn])


def _moe_experts(h, tok_sorted, unit_e, unit_start, unit_n, w_gate, w_up, w_down):
    m, d = h.shape
    n_units = unit_e.shape[0]
    de = w_gate.shape[2]
    tf = _tile(de, 256, LANES)
    nf = de // tf

    def fidx(u, f, un):
        return jnp.where(un[u] > 0, f, nf - 1)

    return pl.pallas_call(
        _moe_kernel,
        grid_spec=pltpu.PrefetchScalarGridSpec(
            num_scalar_prefetch=4,
            grid=(n_units, nf),
            in_specs=[pl.BlockSpec(memory_space=pl.ANY),
                      pl.BlockSpec((1, d, tf), lambda u, f, ue, us, un, tok: (ue[u], 0, fidx(u, f, un))),
                      pl.BlockSpec((1, d, tf), lambda u, f, ue, us, un, tok: (ue[u], 0, fidx(u, f, un))),
                      pl.BlockSpec((1, tf, d), lambda u, f, ue, us, un, tok: (ue[u], fidx(u, f, un), 0))],
            out_specs=pl.BlockSpec((MOE_UNIT, d), lambda u, f, ue, us, un, tok: (u, 0)),
            scratch_shapes=[pltpu.VMEM((MOE_UNIT, d), F32), pltpu.VMEM((MOE_UNIT, d), BF16),
                            pltpu.VMEM((d, tf), BF16), pltpu.VMEM((d, tf), BF16),
                            pltpu.VMEM((tf, d), BF16), pltpu.SemaphoreType.DMA(())]),
        out_shape=jax.ShapeDtypeStruct((n_units * MOE_UNIT, d), F32),
        compiler_params=_params(("arbitrary", "arbitrary")),
        name="moe_experts",
    )(unit_e, unit_start, unit_n, tok_sorted, h, w_gate, w_up, w_down)


def _combine_kernel(pos_ref, ys_hbm, w_ref, x_ref, g_ref, o_ref, buf, sems, *, normalize):
    i = pl.program_id(0)
    tc = x_ref.shape[0]
    half = tc // 2
    base = i * tc * TOP_K

    def row_copy(p, s, t, part):
        return pltpu.make_async_copy(ys_hbm.at[pl.ds(p, 1), :], buf.at[s, pl.ds(t, 1), :], sems.at[part])

    def start(part):
        def body(t, c):
            for s in range(TOP_K):
                row_copy(pos_ref[base + TOP_K * t + s], s, t, part).start()
            return c
        lax.fori_loop(part * half, (part + 1) * half, body, 0)

    def wait(part):
        def body(t, c):
            for s in range(TOP_K):
                row_copy(0, s, t, part).wait()
            return c
        lax.fori_loop(part * half, (part + 1) * half, body, 0)

    def finish(part):
        rows = slice(part * half, (part + 1) * half)
        w = w_ref[rows, :]
        x = x_ref[rows, :] + (w[:, 0:1] * buf[0, rows, :] + w[:, 1:2] * buf[1, rows, :])
        if normalize:
            ms = jnp.mean(x * x, axis=-1, keepdims=True)
            x = x * lax.rsqrt(ms + RMS_EPS) * g_ref[...]
        o_ref[rows, :] = x

    start(0)
    start(1)
    wait(0)
    finish(0)
    wait(1)
    finish(1)


def _moe_combine(ys, pos, wts, x, g, normalize):
    m, d = x.shape
    tc = _tile(m, 128, 16)
    return pl.pallas_call(
        functools.partial(_combine_kernel, normalize=normalize),
        grid_spec=pltpu.PrefetchScalarGridSpec(
            num_scalar_prefetch=1,
            grid=(m // tc,),
            in_specs=[pl.BlockSpec(memory_space=pl.ANY),
                      pl.BlockSpec((tc, LANES), lambda i, p: (i, 0)),
                      pl.BlockSpec((tc, d), lambda i, p: (i, 0)),
                      pl.BlockSpec((1, d), lambda i, p: (0, 0))],
            out_specs=pl.BlockSpec((tc, d), lambda i, p: (i, 0)),
            scratch_shapes=[pltpu.VMEM((TOP_K, tc, d), F32), pltpu.SemaphoreType.DMA((2,))]),
        out_shape=jax.ShapeDtypeStruct((m, d), F32),
        compiler_params=_params(("arbitrary",)),
        name="moe_combine",
    )(pos, ys, wts, x, g.reshape(1, d))


def _moe_layer(x, norm_g, rg_w, rg_b, re_w, re_b, w_gate, w_up, w_down, final_g, normalize):
    m, d = x.shape
    n_exp = re_w.shape[1]
    h, expert, wts = _router(x, norm_g, rg_w, rg_b, re_w, re_b)

    n_rows = m * TOP_K
    flat_e = expert.reshape(-1)
    order = jnp.argsort(flat_e).astype(jnp.int32)
    e_sorted = flat_e[order]
    tok_sorted = order // TOP_K
    counts = jnp.zeros((n_exp,), jnp.int32).at[flat_e].add(1)
    start = jnp.cumsum(counts) - counts
    units_e = (counts + MOE_UNIT - 1) // MOE_UNIT
    ucum = jnp.cumsum(units_e)
    ufirst = ucum - units_e
    n_units = n_exp + -(-n_rows // MOE_UNIT)
    uid = jnp.arange(n_units, dtype=jnp.int32)
    total = ucum[-1]
    ue = jnp.minimum(jnp.searchsorted(ucum, uid, side="right"), n_exp - 1).astype(jnp.int32)
    local = uid - ufirst[ue]
    active = uid < total
    un = jnp.where(active, jnp.clip(counts[ue] - local * MOE_UNIT, 0, MOE_UNIT), 0).astype(jnp.int32)
    us = jnp.where(active, start[ue] + local * MOE_UNIT, 0).astype(jnp.int32)
    last_e = ue[jnp.maximum(total - 1, 0)]
    ue = jnp.where(active, ue, last_e).astype(jnp.int32)
    local_row = jnp.arange(n_rows, dtype=jnp.int32) - start[e_sorted]
    pos_sorted = (ufirst[e_sorted] + local_row // MOE_UNIT) * MOE_UNIT + local_row % MOE_UNIT
    pos = jnp.zeros((n_rows,), jnp.int32).at[order].set(pos_sorted.astype(jnp.int32))

    ys = _moe_experts(h, tok_sorted, ue, us, un, w_gate, w_up, w_down)
    return _moe_combine(ys, pos, wts, x, final_g, normalize)


def _mixer(x, shift0, wkv0, past, lw):
    b, t, d = x.shape
    rw_proj = lw["mu"].shape[0]
    rww = lw["w_up_a"].shape[0]
    sbw = lw["w_up_b"].shape[0]
    heads = lw["sb_bias"].shape[0]
    hd = sbw // heads
    q_off, k_off, v_off = rw_proj, rw_proj + sbw, rw_proj + 2 * sbw
    ga_off, gb_off = rw_proj + 3 * sbw, rw_proj + 3 * sbw + d
    x2 = x.reshape(b * t, d)

    h = _rmsnorm(x2, lw["norm1_g"], BF16)
    proj = _matmul(h, lw["w_in"], F32, 1024, 768)
    proj3 = proj.reshape(b, t, -1)

    r, lgw, k2, v, na, bb, bonus = _rwkv_prep(proj3, shift0, lw["mu"], lw["w0"], lw["w2"], lw["a0"],
                                              lw["a2"], lw["kk"], lw["ka"], lw["rk"])
    o_a, wkv_new = _rwkv_recurrence(r, lgw, k2, v, na, bb, bonus, wkv0, lw["ln_g"], lw["ln_b"])
    shift_new = proj3[:, -1, :rw_proj]

    k_new = proj3[:, :, k_off:k_off + sbw]
    v_new = proj3[:, :, v_off:v_off + sbw]
    if past is None:
        o_b = _sb_prompt(proj3, lw["sb_bias"], q_off, k_off, v_off, heads, hd)
    else:
        cache_k, cache_v, layer, page_table = past
        q = proj3[:, :, q_off:q_off + sbw]
        o_b = _sb_sample(q, k_new, v_new, cache_k, cache_v, layer, page_table, lw["sb_bias"]).astype(BF16)

    mixed = _merge_up(o_a.reshape(b * t, rww), o_b.reshape(b * t, sbw), proj, lw["w_up_a"], lw["w_up_b"],
                      ga_off, gb_off)
    x1 = _out_proj(mixed, lw["w_out"], x2)
    return (x1, k_new.reshape(b, t, heads, hd), v_new.reshape(b, t, heads, hd), wkv_new, shift_new)


def kernel(x_prompt, x_sample, cache_k, cache_v, state_wkv, state_shift, page_table, norm1_g, w_in, rwkv_mu, rwkv_w0, rwkv_w2, rwkv_a0, rwkv_a2, rwkv_kk, rwkv_ka, rwkv_rk, rwkv_ln_g, rwkv_ln_b, sb_bias, w_up_a, w_up_b, w_out, norm2_g, router_group_w, router_group_b, router_expert_w, router_expert_b, expert_w_gate, expert_w_up, expert_w_down, normf_g):
    depth = w_in.shape[0]
    bp, tp, d = x_prompt.shape
    bs, ts, _ = x_sample.shape
    rw_proj = rwkv_mu.shape[1]
    rww = w_up_a.shape[1]
    rw_heads = rww // RW_HEAD_DIM
    xp, xs = x_prompt, x_sample
    outs = [[] for _ in range(8)]
    for l in range(depth):
        lw = dict(norm1_g=norm1_g[l], w_in=w_in[l].astype(BF16), mu=rwkv_mu[l], w0=rwkv_w0[l],
                  w2=rwkv_w2[l], a0=rwkv_a0[l], a2=rwkv_a2[l], kk=rwkv_kk[l], ka=rwkv_ka[l],
                  rk=rwkv_rk[l].reshape(-1), ln_g=rwkv_ln_g[l], ln_b=rwkv_ln_b[l], sb_bias=sb_bias[l],
                  w_up_a=w_up_a[l].astype(BF16), w_up_b=w_up_b[l].astype(BF16),
                  w_out=w_out[l].astype(BF16))
        shift0 = jnp.zeros((bp, rw_proj), F32)
        wkv0 = jnp.zeros((bp, rw_heads, RW_HEAD_DIM, RW_HEAD_DIM), F32)
        res_p = _mixer(xp, shift0, wkv0, None, lw)
        res_s = _mixer(xs, state_shift[l], state_wkv[l], (cache_k, cache_v, l, page_table), lw)
        for i in range(4):
            outs[i].append(res_p[1 + i])
            outs[4 + i].append(res_s[1 + i])
        x1 = jnp.concatenate([res_p[0], res_s[0]], axis=0)
        last = l == depth - 1
        x2 = _moe_layer(x1, norm2_g[l], router_group_w[l], router_group_b[l], router_expert_w[l],
                        router_expert_b[l], expert_w_gate[l], expert_w_up[l], expert_w_down[l],
                        normf_g, last)
        xp = x2[:bp * tp].reshape(bp, tp, d)
        xs = x2[bp * tp:].reshape(bs, ts, d)
    return (xp, xs) + tuple(jnp.stack(o) for o in outs)
```

```python
import functools

import jax
import jax.numpy as jnp
from jax import lax
from jax.experimental import pallas as pl
from jax.experimental.pallas import tpu as pltpu

F32 = jnp.float32
BF16 = jnp.bfloat16

LANES = 128
RW_HEAD_DIM = 64
RW_CHUNK = 64
RW_INV_BLOCK = 16
RW_PAIRS_PER_STEP = 16
RW_GN_EPS = 64e-5
RMS_EPS = 1e-6
TOP_K = 2
MOE_SUB = 128
MOE_UNIT = 3 * MOE_SUB
MOE_DOWN_COLS = 512
VMEM_LIMIT_MB = 56


def _params(sem, vmem_mb=VMEM_LIMIT_MB):
    return pltpu.CompilerParams(dimension_semantics=sem, vmem_limit_bytes=vmem_mb << 20)


def _tile(n, pref, mult=8):
    t = min(pref, n)
    while t >= mult:
        if n % t == 0 and t % mult == 0:
            return t
        t -= 1
    return n


def _dg(a, b, nt=False):
    if a.ndim == 3:
        dims = (((2,), (2 if nt else 1,)), ((0,), (0,)))
    else:
        dims = (((1,), (1 if nt else 0,)), ((), ()))
    return lax.dot_general(a, b, dims, preferred_element_type=F32)


def _split(x):
    hi = x.astype(BF16)
    lo = (x - hi.astype(F32)).astype(BF16)
    return hi, lo


def _dot1(a, b, nt=False):
    return _dg(a.astype(BF16), b.astype(BF16), nt)


def _dot3(a, b, nt=False):
    ah, al = _split(a)
    bh, bl = _split(b)
    return _dg(ah, bh, nt) + (_dg(ah, bl, nt) + _dg(al, bh, nt))


def _dot2x(a, b_exact):
    ah, al = _split(a)
    return _dg(ah, b_exact) + _dg(al, b_exact)


def _sigmoid(x):
    return 1.0 / (1.0 + jnp.exp(-x))


def _softplus(x):
    return jnp.maximum(x, 0.0) + jnp.log1p(jnp.exp(-jnp.abs(x)))


def _softplus_pair(z):
    l = jnp.log(1.0 + jnp.exp(-jnp.abs(z)))
    return jnp.maximum(z, 0.0) + l, jnp.minimum(z, 0.0) - l


def _rmsnorm_kernel(x_ref, g_ref, o_ref):
    x = x_ref[...]
    ms = jnp.mean(x * x, axis=-1, keepdims=True)
    o_ref[...] = (x * lax.rsqrt(ms + RMS_EPS) * g_ref[...]).astype(o_ref.dtype)


def _rmsnorm(x, g, out_dtype):
    m, d = x.shape
    tm = _tile(m, 256)
    return pl.pallas_call(
        _rmsnorm_kernel,
        grid=(m // tm,),
        in_specs=[pl.BlockSpec((tm, d), lambda i: (i, 0)),
                  pl.BlockSpec((1, d), lambda i: (0, 0))],
        out_specs=pl.BlockSpec((tm, d), lambda i: (i, 0)),
        out_shape=jax.ShapeDtypeStruct((m, d), out_dtype),
        compiler_params=_params(("parallel",)),
        name="rmsnorm",
    )(x, g.reshape(1, d))


def _mm_kernel(a_ref, b_ref, o_ref):
    o_ref[...] = jnp.dot(a_ref[...], b_ref[...], preferred_element_type=F32).astype(o_ref.dtype)


def _matmul(a, b, out_dtype, tm_pref, tn_pref):
    m, k = a.shape
    n = b.shape[1]
    tm = _tile(m, tm_pref, 16)
    tn = _tile(n, tn_pref, LANES)
    return pl.pallas_call(
        _mm_kernel,
        grid=(m // tm, n // tn),
        in_specs=[pl.BlockSpec((tm, k), lambda i, j: (i, 0)),
                  pl.BlockSpec((k, tn), lambda i, j: (0, j))],
        out_specs=pl.BlockSpec((tm, tn), lambda i, j: (i, j)),
        out_shape=jax.ShapeDtypeStruct((m, n), out_dtype),
        compiler_params=_params(("parallel", "arbitrary")),
        name="in_proj",
    )(a, b)


def _merge_kernel(oa_ref, ob_ref, wa_ref, wb_ref, ga_ref, gb_ref, o_ref):
    ya = jnp.dot(oa_ref[...], wa_ref[...], preferred_element_type=F32)
    yb = jnp.dot(ob_ref[...], wb_ref[...], preferred_element_type=F32)
    o_ref[...] = (_sigmoid(ga_ref[...]) * ya + _sigmoid(gb_ref[...]) * yb).astype(o_ref.dtype)


def _merge_up(o_a, o_b, proj, w_up_a, w_up_b, ga_off, gb_off):
    m, ka = o_a.shape
    kb = o_b.shape[1]
    d = w_up_a.shape[1]
    tm = _tile(m, 1024, 16)
    tn = 256
    assert d % tn == 0 and ga_off % tn == 0 and gb_off % tn == 0
    ga_blk, gb_blk = ga_off // tn, gb_off // tn
    return pl.pallas_call(
        _merge_kernel,
        grid=(m // tm, d // tn),
        in_specs=[pl.BlockSpec((tm, ka), lambda i, j: (i, 0)),
                  pl.BlockSpec((tm, kb), lambda i, j: (i, 0)),
                  pl.BlockSpec((ka, tn), lambda i, j: (0, j)),
                  pl.BlockSpec((kb, tn), lambda i, j: (0, j)),
                  pl.BlockSpec((tm, tn), lambda i, j: (i, ga_blk + j)),
                  pl.BlockSpec((tm, tn), lambda i, j: (i, gb_blk + j))],
        out_specs=pl.BlockSpec((tm, tn), lambda i, j: (i, j)),
        out_shape=jax.ShapeDtypeStruct((m, d), BF16),
        compiler_params=_params(("parallel", "arbitrary")),
        name="merge_up",
    )(o_a, o_b, w_up_a, w_up_b, proj, proj)


def _outproj_kernel(a_ref, b_ref, x_ref, o_ref):
    o_ref[...] = x_ref[...] + jnp.dot(a_ref[...], b_ref[...], preferred_element_type=F32)


def _out_proj(mixed, w_out, x):
    m, k = mixed.shape
    n = w_out.shape[1]
    tm = _tile(m, 1024, 16)
    tn = _tile(n, 512, LANES)
    return pl.pallas_call(
        _outproj_kernel,
        grid=(m // tm, n // tn),
        in_specs=[pl.BlockSpec((tm, k), lambda i, j: (i, 0)),
                  pl.BlockSpec((k, tn), lambda i, j: (0, j)),
                  pl.BlockSpec((tm, tn), lambda i, j: (i, j))],
        out_specs=pl.BlockSpec((tm, tn), lambda i, j: (i, j)),
        out_shape=jax.ShapeDtypeStruct((m, n), F32),
        compiler_params=_params(("parallel", "arbitrary")),
        name="out_proj",
    )(mixed, w_out, x)


def _rwkv_prep_kernel(p_ref, sh_ref, mu_ref, w0_ref, w2_ref, a0_ref, a2_ref, kkw_ref, kaw_ref,
                      rkw_ref, ones_ref,
                      r_out, lw_out, k_out, v_out, na_out, bb_out, bon_out,
                      carry, *, rww, dl, al):
    t = pl.program_id(1)
    tt = p_ref.shape[1]

    @pl.when(t == 0)
    def _():
        carry[0:1, :] = sh_ref[0]

    row0 = lax.broadcasted_iota(jnp.int32, (tt, 1), 0) == 0

    def mixed(lo, hi):
        p = p_ref[0, :, lo:hi]
        prev = jnp.where(row0, carry[0:1, lo:hi], pltpu.roll(p, 1, 0))
        return p + (prev - p) * mu_ref[:, lo:hi]

    wd = mixed(3 * rww, 3 * rww + dl)
    ad = mixed(3 * rww + dl, 3 * rww + dl + al)
    w_lora = _dot3(jnp.tanh(wd), w2_ref[...])
    a_lora = _dot3(ad, a2_ref[...])
    ones = ones_ref[...]
    for s in range(rww // LANES):
        c0, c1 = s * LANES, (s + 1) * LANES
        r = mixed(c0, c1)
        k = mixed(rww + c0, rww + c1)
        v = mixed(2 * rww + c0, 2 * rww + c1)
        w = w0_ref[:, c0:c1] + w_lora[:, c0:c1]
        lw = -jnp.exp(-_softplus(-w) - 0.5)
        a = _sigmoid(a0_ref[:, c0:c1] + a_lora[:, c0:c1])
        kk = k * kkw_ref[:, c0:c1]
        ss = _dot2x(kk * kk, ones)
        kk = kk / jnp.maximum(jnp.sqrt(ss), 1e-12)
        k2 = k * (1.0 + (a - 1.0) * kaw_ref[:, c0:c1])
        rk = _dot2x(r * k2 * rkw_ref[:, c0:c1], ones)
        r_out[0, :, c0:c1] = r
        lw_out[0, :, c0:c1] = lw
        k_out[0, :, c0:c1] = k2
        v_out[0, :, c0:c1] = v
        na_out[0, :, c0:c1] = -kk
        bb_out[0, :, c0:c1] = kk * a
        bon_out[0, :, c0:c1] = rk * v
    carry[0:1, :] = p_ref[0, tt - 1:tt, :]


def _head_ones():
    i = jnp.arange(LANES) // RW_HEAD_DIM
    return (i[:, None] == i[None, :]).astype(BF16)


def _rwkv_prep(proj, shift0, mu, w0, w2, a0, a2, kkw, kaw, rkw):
    b, t, _ = proj.shape
    p = mu.shape[0]
    rww = w0.shape[0]
    dl, al = w2.shape[0], a2.shape[0]
    tt = _tile(t, 128)
    row = lambda x: x.reshape(1, -1)
    out = jax.ShapeDtypeStruct((b, t, rww), F32)
    vec = lambda n: pl.BlockSpec((1, n), lambda i, j: (0, 0))
    full = lambda s: pl.BlockSpec(s, lambda i, j: (0, 0))
    blk = pl.BlockSpec((1, tt, rww), lambda i, j: (i, j, 0))
    return pl.pallas_call(
        functools.partial(_rwkv_prep_kernel, rww=rww, dl=dl, al=al),
        grid=(b, t // tt),
        in_specs=[pl.BlockSpec((1, tt, p), lambda i, j: (i, j, 0)),
                  pl.BlockSpec((1, 1, p), lambda i, j: (i, 0, 0)),
                  vec(p), vec(rww), full((dl, rww)), vec(rww), full((al, rww)),
                  vec(rww), vec(rww), vec(rww), full((LANES, LANES))],
        out_specs=[blk] * 7,
        out_shape=[out] * 7,
        scratch_shapes=[pltpu.VMEM((8, p), F32)],
        compiler_params=_params(("parallel", "arbitrary")),
        name="rwkv_prep",
    )(proj, shift0.reshape(b, 1, p), row(mu), row(w0), w2, row(a0), a2, row(kkw), row(kaw),
      row(rkw), _head_ones())


def _tri_inv(a, n, nb):
    row = lax.broadcasted_iota(jnp.int32, (n, n), 0)
    col = lax.broadcasted_iota(jnp.int32, (n, n), 1)
    dmask = (row // RW_INV_BLOCK) == (col // RW_INV_BLOCK)
    eye = jnp.where(row == col, 1.0, 0.0)
    ad = jnp.where(dmask, a, 0.0)
    ao = jnp.where(dmask, 0.0, a)
    dinv = eye + ad
    p = ad
    k = 1
    while 2 * k < RW_INV_BLOCK:
        p = _dot1(p, p)
        dinv = dinv + _dot1(dinv, p)
        k *= 2
    bm = _dot1(dinv, ao)
    x = eye + bm
    p = bm
    k = 1
    while 2 * k < nb:
        p = _dot1(p, p)
        x = x + _dot1(x, p)
        k *= 2
    return _dot1(x, dinv)


def _rwkv_chunk_math(r, cum, lw, kk, v, na, bb, s):
    ch = r.shape[1]
    n = 2 * ch
    cl = cum[:, ch - 1:ch, :]
    e_neg = jnp.exp(-cum)
    e_end = jnp.exp(cl - cum)
    a_t = na * jnp.exp(cum - lw)
    r_t = r * jnp.exp(cum)
    lane_lo = lax.broadcasted_iota(jnp.int32, (ch, LANES), 1) < RW_HEAD_DIM

    def hat(x):
        return jnp.concatenate([jnp.where(lane_lo, x, 0.0), jnp.where(lane_lo, 0.0, x)], axis=1)

    ath, rth, vh = hat(a_t), hat(r_t), hat(v)
    btkt = jnp.concatenate([hat(bb * e_neg), hat(kk * e_neg)], axis=1)
    bbh, kbh = hat(bb * e_end), hat(kk * e_end)

    row = lax.broadcasted_iota(jnp.int32, (n, n), 0)
    col = lax.broadcasted_iota(jnp.int32, (n, n), 1)
    strict = row > col
    incl = row >= col
    sc = _dot1(jnp.concatenate([ath, rth], axis=1), btkt, nt=True)
    a_ab = jnp.where(strict, sc[:, :n, :n], 0.0)
    a_ak = jnp.where(strict, sc[:, :n, n:], 0.0)
    a_rb = jnp.where(incl, sc[:, n:, :n], 0.0)
    a_rk = jnp.where(incl, sc[:, n:, n:], 0.0)

    tinv = _tri_inv(a_ab, n, ch // RW_INV_BLOCK)
    x = _dot1(tinv, jnp.concatenate([_dot1(a_ak, vh), ath], axis=2))
    u0, a2 = x[:, :, :LANES], x[:, :, LANES:]
    z = _dot1(a_rb, x)
    y0 = z[:, :, :LANES] + _dot1(a_rk, vh)
    rh = rth + z[:, :, LANES:]
    yh = _dot1(rh, s, nt=True) + y0
    y = yh[:, :ch] + yh[:, ch:]

    tr = lambda t: jnp.swapaxes(t, 1, 2)
    m = jnp.where(row == col, jnp.exp(cl), 0.0) + _dot1(tr(a2), bbh)
    nn = _dot1(jnp.concatenate([tr(u0), tr(vh)], axis=2), jnp.concatenate([bbh, kbh], axis=1))
    return y, _dot3(s, m) + nn


def _rwkv_chunk_kernel(r_ref, lw_ref, k_ref, v_ref, na_ref, bb_ref, bon_ref, lng_ref, lnb_ref,
                       s0_ref, ltri_ref, ones_ref, o_ref, sf_ref, s_scr):
    c = pl.program_id(2)
    pairs = s_scr.shape[0]
    ch = r_ref.shape[1]

    @pl.when(c == 0)
    def _():
        s_scr[...] = s0_ref[0]

    def pairwise(x):
        return jnp.stack([x[:, g * LANES:(g + 1) * LANES] for g in range(pairs)], axis=0)

    lw = lw_ref[0]
    cum = _dot2x_lhs(ltri_ref[...], lw)
    y, s_new = _rwkv_chunk_math(pairwise(r_ref[0]), pairwise(cum), pairwise(lw), pairwise(k_ref[0]),
                                pairwise(v_ref[0]), pairwise(na_ref[0]), pairwise(bb_ref[0]), s_scr[...])
    s_scr[...] = s_new
    sf_ref[0] = s_new

    ones = ones_ref[...]
    y2 = y.reshape(pairs * ch, LANES)
    mean = _dot2x(y2, ones) * (1.0 / RW_HEAD_DIM)
    d = y2 - mean
    var = _dot2x(d * d, ones) * (1.0 / RW_HEAD_DIM)
    yn = d * lax.rsqrt(var + RW_GN_EPS)
    for g in range(pairs):
        sl = slice(g * LANES, (g + 1) * LANES)
        out = yn[g * ch:(g + 1) * ch] * lng_ref[:, sl] + lnb_ref[:, sl] + bon_ref[0, :, sl]
        o_ref[0, :, sl] = out.astype(o_ref.dtype)


def _dot2x_lhs(a_exact, b):
    bh, bl = _split(b)
    return _dg(a_exact, bh) + _dg(a_exact, bl)


def _rwkv_recurrence(r, lw, k, v, na, bb, bonus, wkv0, ln_g, ln_b):
    b, t, rww = r.shape
    hp = rww // LANES
    hd = RW_HEAD_DIM
    ch = RW_CHUNK
    tp = -(-t // ch) * ch
    if tp != t:
        pad = lambda x: jnp.pad(x, ((0, 0), (0, tp - t), (0, 0)))
        r, lw, k, v, na, bb, bonus = map(pad, (r, lw, k, v, na, bb, bonus))
    nc = tp // ch
    w0 = wkv0.astype(F32).reshape(b, hp, 2, hd, hd)
    zero = jnp.zeros((b, hp, hd, hd), F32)
    s0 = jnp.concatenate([jnp.concatenate([w0[:, :, 0], zero], axis=-1),
                          jnp.concatenate([zero, w0[:, :, 1]], axis=-1)], axis=-2)
    ltri = (jnp.arange(ch)[:, None] >= jnp.arange(ch)[None, :]).astype(BF16)
    pairs = _tile(hp, RW_PAIRS_PER_STEP, 1)
    blk = pl.BlockSpec((1, ch, pairs * LANES), lambda i, j, c: (i, c, j))
    vec = pl.BlockSpec((1, pairs * LANES), lambda i, j, c: (0, j))
    st = pl.BlockSpec((1, pairs, LANES, LANES), lambda i, j, c: (i, j, 0, 0))
    full = lambda s: pl.BlockSpec(s, lambda i, j, c: (0, 0))
    o, sf = pl.pallas_call(
        _rwkv_chunk_kernel,
        grid=(b, hp // pairs, nc),
        in_specs=[blk] * 7 + [vec, vec, st, full((ch, ch)), full((LANES, LANES))],
        out_specs=[blk, st],
        out_shape=[jax.ShapeDtypeStruct((b, tp, rww), BF16),
                   jax.ShapeDtypeStruct((b, hp, LANES, LANES), F32)],
        scratch_shapes=[pltpu.VMEM((pairs, LANES, LANES), F32)],
        compiler_params=_params(("parallel", "parallel", "arbitrary")),
        name="rwkv_chunk",
    )(r, lw, k, v, na, bb, bonus, ln_g.reshape(1, rww), ln_b.reshape(1, rww), s0, ltri, _head_ones())
    sf = sf.reshape(b, hp, 2, hd, 2, hd)
    wkv = jnp.stack([sf[:, :, 0, :, 0, :], sf[:, :, 1, :, 1, :]], axis=2).reshape(b, 2 * hp, hd, hd)
    return o[:, :t], wkv


def _suffix_matrix(tk):
    i = jnp.arange(tk)
    upper = (i[:, None] > i[None, :]).astype(BF16)
    return jnp.concatenate([upper, jnp.ones((tk, tk), BF16)], axis=1)


def _sb_prompt_kernel(bias_ref, q_ref, k_ref, v_ref, mxo_ref, o_ref, acc, carry, *, scale, hd, nh):
    qi = pl.program_id(2)
    tq = q_ref.shape[1]
    mxo = mxo_ref[...]

    def heads_of(ref, rows):
        return jnp.stack([ref[0, rows, i * hd:(i + 1) * hd] for i in range(nh)], axis=0)

    q = (heads_of(q_ref, slice(None)) * scale).astype(BF16)
    m = nh * tq

    def scores(rows):
        z = _dg(q, heads_of(k_ref, rows).astype(BF16), nt=True).reshape(m, rows.size)
        bias = jnp.concatenate([jnp.broadcast_to(bias_ref[i, 0:1, :rows.size], (tq, rows.size))
                                for i in range(nh)], axis=0)
        return _softplus_pair(z + bias)

    def weighted_values(att, rows):
        pv = _dg(att.astype(BF16).reshape(nh, tq, rows.size), heads_of(v_ref, rows).astype(BF16))
        return pv.reshape(m, hd)

    def rows_at(kb, n):
        return pl.ds(pl.multiple_of(kb * tq, tq), n)

    row = lax.broadcasted_iota(jnp.int32, (m, tq), 0) % tq
    col = lax.broadcasted_iota(jnp.int32, (m, tq), 1)
    mask = col < row
    rows = rows_at(qi, tq)
    sp, la = scores(rows)
    lsm = jnp.where(mask, -sp, 0.0)
    suf = _dg(lsm.astype(BF16), mxo)
    att = jnp.where(mask, jnp.exp(la + suf), 0.0)
    acc[...] = weighted_values(att, rows)
    carry[...] = jnp.broadcast_to(suf[:, 0:1] + lsm[:, 0:1], (m, hd))

    def wide(kb, c):
        rows = rows_at(kb - 1, 2 * tq)
        sp, la = scores(rows)
        lsm = -sp
        lsb = lsm.astype(BF16)
        suf = _dg(jnp.concatenate([lsb[:, :tq], lsb[:, tq:]], axis=0), mxo)
        suf_old, suf_new = suf[:m], suf[m:]
        tot_new = suf_new[:, 0:1] + lsm[:, tq:tq + 1]
        tot_old = suf_old[:, 0:1] + lsm[:, 0:1]
        att_old = jnp.exp(la[:, :tq] + suf_old + tot_new)
        att_new = jnp.exp(la[:, tq:] + suf_new)
        pv = weighted_values(jnp.concatenate([att_old, att_new], axis=1), rows)
        cr = carry[...]
        acc[...] += jnp.exp(cr) * pv
        carry[...] = cr + jnp.broadcast_to(tot_old + tot_new, (m, hd))
        return c

    lax.fori_loop(0, qi // 2, lambda i, c: wide(qi - 1 - 2 * i, c), 0)

    @pl.when(qi % 2 == 1)
    def _():
        rows = rows_at(0, tq)
        sp, la = scores(rows)
        att = jnp.exp(la + _dg((-sp).astype(BF16), mxo))
        acc[...] += jnp.exp(carry[...]) * weighted_values(att, rows)

    for i in range(nh):
        o_ref[0, :, i * hd:(i + 1) * hd] = acc[i * tq:(i + 1) * tq, :].astype(o_ref.dtype)


def _sb_prompt(proj, bias, q_off, k_off, v_off, heads, hd):
    b, t, _ = proj.shape
    tq = _tile(t, 256, LANES)
    nh = _tile(heads, SB_HEADS_PER_STEP, 1)
    w = nh * hd
    assert q_off % w == 0 and k_off % w == 0 and v_off % w == 0
    qb, kb, vb = q_off // w, k_off // w, v_off // w
    bias_rows = jnp.broadcast_to(bias.astype(F32)[:, None, None], (heads, 8, 2 * tq))
    return pl.pallas_call(
        functools.partial(_sb_prompt_kernel, scale=hd ** -0.5, hd=hd, nh=nh),
        grid=(b, heads // nh, t // tq),
        in_specs=[pl.BlockSpec((nh, 8, 2 * tq), lambda i, h, j: (h, 0, 0)),
                  pl.BlockSpec((1, tq, w), lambda i, h, j: (i, j, qb + h)),
                  pl.BlockSpec((1, t, w), lambda i, h, j: (i, 0, kb + h)),
                  pl.BlockSpec((1, t, w), lambda i, h, j: (i, 0, vb + h)),
                  pl.BlockSpec((tq, tq), lambda i, h, j: (0, 0))],
        out_specs=pl.BlockSpec((1, tq, w), lambda i, h, j: (i, j, h)),
        out_shape=jax.ShapeDtypeStruct((b, t, heads * hd), BF16),
        scratch_shapes=[pltpu.VMEM((nh * tq, hd), F32), pltpu.VMEM((nh * tq, hd), F32)],
        compiler_params=_params(("parallel", "parallel", "arbitrary")),
        name="sb_prompt",
    )(bias_rows, proj, proj, proj, _suffix_matrix(tq)[:, :tq])


SB_HEADS_PER_STEP = 2
SB_QPAD = 16
SB_PAGES_PER_STEP = 8


def _sb_sample_kernel(pt_ref, q_ref, *refs, heads, hd, tq, scale, pps, hg):
    ng = heads // hg
    group = lambda rs: [rs[i * ng:(i + 1) * ng] for i in range(len(rs) // ng)]
    kn_refs, vn_refs = group(refs[:ng]), group(refs[ng:2 * ng])
    refs = refs[2 * ng:]
    kp_refs, vp_refs = group(refs[:pps * ng]), group(refs[pps * ng:2 * pps * ng])
    bias_ref, mxo_ref, o_ref, zbuf, abuf, pvbuf, acc, carry = refs[2 * pps * ng:]
    j = pl.program_id(1)
    nj = pl.num_programs(1)
    qp = SB_QPAD
    rows = heads * qp

    def pages(k_refs, v_refs, masked):
        n = len(k_refs)
        tk = k_refs[0][0].shape[1]

        def head_rows(page_refs, h):
            ref = page_refs[h // hg].reshape(tk * hg, hd)
            return ref[pl.ds(h % hg, tk, stride=hg), :].astype(BF16)

        qs = []
        for h in range(heads):
            qh = jnp.concatenate([q_ref[0, :, h * hd:(h + 1) * hd] * scale,
                                  jnp.zeros((qp - tq, hd), F32)], axis=0)
            qs.append(qh.astype(BF16))
        for s in range(n):
            for h in range(heads):
                zbuf[s, h * qp:(h + 1) * qp, :] = _dg(qs[h], head_rows(k_refs[s], h), nt=True)
        sp, la = _softplus_pair(zbuf[0:n] + bias_ref[...])
        if masked:
            qidx = lax.broadcasted_iota(jnp.int32, (rows, tk), 0) % qp
            kidx = lax.broadcasted_iota(jnp.int32, (rows, tk), 1)
            mask = kidx < qidx
            lsm = jnp.where(mask, -sp, 0.0)
        else:
            lsm = -sp
        st = _dg(lsm.astype(BF16).reshape(n * rows, tk), mxo_ref[...]).reshape(n, rows, 2 * tk)
        att = jnp.exp(la + st[:, :, :tk])
        if masked:
            att = jnp.where(mask, att, 0.0)
        abuf[0:n] = att
        for s in range(n):
            for h in range(heads):
                ah = abuf[s, h * qp:(h + 1) * qp, :].astype(BF16)
                pvbuf[s, h * qp:(h + 1) * qp, :] = _dg(ah, head_rows(v_refs[s], h))
        return st[:, :, tk:tk + hd]

    @pl.when(j == 0)
    def _():
        carry[...] = pages(kn_refs, vn_refs, True)[0]
        acc[...] = pvbuf[0]

    @pl.when(j > 0)
    def _():
        tots = pages(kp_refs, vp_refs, False)
        cr = carry[...]
        out = acc[...]
        for s in range(pps):
            out = out + jnp.exp(cr) * pvbuf[s]
            cr = cr + tots[s]
        acc[...] = out
        carry[...] = cr

    @pl.when(j == nj - 1)
    def _():
        for h in range(heads):
            o_ref[0, :, h * hd:(h + 1) * hd] = acc[h * qp:h * qp + tq, :]


def _sb_sample(q, k_new, v_new, cache_k, cache_v, layer, page_table, bias):
    nb, tq, sbw = q.shape
    depth, nphys, page, heads, hd = cache_k.shape
    n_pages = page_table.shape[1]
    assert tq <= SB_QPAD and tq <= page and page % LANES == 0
    hg = 8 if heads % 8 == 0 else heads
    ng = heads // hg
    pad = lambda x: jnp.pad(x.reshape(nb, tq, heads, hd), ((0, 0), (0, page - tq), (0, 0), (0, 0)))
    ck = cache_k.reshape(depth * nphys, page, heads, hd)
    cv = cache_v.reshape(depth * nphys, page, heads, hd)
    page_table = page_table + layer * nphys
    rows = heads * SB_QPAD
    bias_rows = jnp.broadcast_to(jnp.repeat(bias.astype(F32), SB_QPAD)[:, None], (rows, page))
    pps = _tile(n_pages, SB_PAGES_PER_STEP, 1)

    def page_map(s, g):
        def index(i, j, pt):
            return (pt[i * n_pages + n_pages - 1 - (jnp.maximum(j, 1) - 1) * pps - s], 0, g, 0)
        return index

    const = lambda i, j, pt: (0, 0)
    blk = (1, page, hg, hd)
    new = [pl.BlockSpec(blk, lambda i, j, pt, g=g: (i, 0, g, 0)) for g in range(ng)]
    pages = [pl.BlockSpec(blk, page_map(s, g)) for s in range(pps) for g in range(ng)]
    return pl.pallas_call(
        functools.partial(_sb_sample_kernel, heads=heads, hd=hd, tq=tq, scale=hd ** -0.5, pps=pps, hg=hg),
        grid_spec=pltpu.PrefetchScalarGridSpec(
            num_scalar_prefetch=1,
            grid=(nb, n_pages // pps + 1),
            in_specs=[pl.BlockSpec((1, tq, sbw), lambda i, j, pt: (i, 0, 0))] + new + new + pages + pages +
                     [pl.BlockSpec((rows, page), const),
                      pl.BlockSpec((page, 2 * page), const)],
            out_specs=pl.BlockSpec((1, tq, sbw), lambda i, j, pt: (i, 0, 0)),
            scratch_shapes=[pltpu.VMEM((pps, rows, page), F32), pltpu.VMEM((pps, rows, page), F32),
                            pltpu.VMEM((pps, rows, hd), F32),
                            pltpu.VMEM((rows, hd), F32), pltpu.VMEM((rows, hd), F32)]),
        out_shape=jax.ShapeDtypeStruct((nb, tq, sbw), F32),
        compiler_params=_params(("parallel", "arbitrary")),
        name="sb_sample",
    )(page_table.reshape(-1).astype(jnp.int32), q, *([pad(k_new)] * ng), *([pad(v_new)] * ng),
      *([ck] * (pps * ng)), *([cv] * (pps * ng)), bias_rows, _suffix_matrix(page))


def _router_kernel(x_ref, g_ref, rw_ref, rb_ref, h_ref, id_ref, wt_ref, *, n_groups, per_group):
    x = x_ref[...]
    ms = jnp.mean(x * x, axis=-1, keepdims=True)
    h = x * lax.rsqrt(ms + RMS_EPS) * g_ref[...]
    h_ref[...] = h
    logits = _dot3(h, rw_ref[...]) + rb_ref[...]
    tm = logits.shape[0]
    lane = lax.broadcasted_iota(jnp.int32, (tm, LANES), 1)
    lane_f = lane.astype(F32)
    neg = jnp.finfo(F32).min
    far = float(LANES)
    is_g = lane < n_groups
    gl = jnp.where(is_g, logits, neg)
    gmax = jnp.max(gl, axis=1, keepdims=True)
    gidx = jnp.min(jnp.where(gl == gmax, lane_f, far), axis=1, keepdims=True).astype(jnp.int32)
    gprob = 1.0 / jnp.sum(jnp.where(is_g, jnp.exp(logits - gmax), 0.0), axis=1, keepdims=True)
    lo = n_groups + gidx * per_group
    el = jnp.where((lane >= lo) & (lane < lo + per_group), logits, neg)
    v1 = jnp.max(el, axis=1, keepdims=True)
    i1 = jnp.min(jnp.where(el == v1, lane_f, far), axis=1, keepdims=True).astype(jnp.int32)
    el2 = jnp.where(lane == i1, neg, el)
    v2 = jnp.max(el2, axis=1, keepdims=True)
    i2 = jnp.min(jnp.where((el2 == v2) & (lane != i1), lane_f, far), axis=1, keepdims=True).astype(jnp.int32)
    t = jnp.exp(v2 - v1)
    w1 = gprob / (1.0 + t)
    w2 = gprob * t / (1.0 + t)
    id_ref[...] = jnp.where(lane == 0, i1 - n_groups, jnp.where(lane == 1, i2 - n_groups, 0))
    wt_ref[...] = jnp.where(lane == 0, w1, jnp.where(lane == 1, w2, 0.0))


def _router(x, g, rg_w, rg_b, re_w, re_b):
    m, d = x.shape
    n_groups = rg_w.shape[1]
    n_exp = re_w.shape[1]
    assert n_groups + n_exp <= LANES
    padc = LANES - n_groups - n_exp
    rw = jnp.concatenate([rg_w, re_w, jnp.zeros((d, padc), F32)], axis=1).astype(F32)
    rb = jnp.concatenate([rg_b, re_b, jnp.zeros((padc,), F32)]).reshape(1, LANES).astype(F32)
    tm = _tile(m, 256)
    row = pl.BlockSpec((tm, d), lambda i: (i, 0))
    nar = pl.BlockSpec((tm, LANES), lambda i: (i, 0))
    h, ids, wts = pl.pallas_call(
        functools.partial(_router_kernel, n_groups=n_groups, per_group=n_exp // n_groups),
        grid=(m // tm,),
        in_specs=[row, pl.BlockSpec((1, d), lambda i: (0, 0)),
                  pl.BlockSpec((d, LANES), lambda i: (0, 0)),
                  pl.BlockSpec((1, LANES), lambda i: (0, 0))],
        out_specs=[row, nar, nar],
        out_shape=[jax.ShapeDtypeStruct((m, d), F32),
                   jax.ShapeDtypeStruct((m, LANES), jnp.int32),
                   jax.ShapeDtypeStruct((m, LANES), F32)],
        compiler_params=_params(("parallel",)),
        name="router",
    )(x, g.reshape(1, d), rw, rb)
    return h, ids[:, :TOP_K], wts


def _moe_kernel(ue_ref, us_ref, un_ref, tok_ref, h_hbm, wg_ref, wu_ref, wd_ref, y_ref,
                xf, xb, wgb, wub, wdb, sem):
    u = pl.program_id(0)
    f = pl.program_id(1)
    n = un_ref[u]

    def row_copy(tok, i):
        return pltpu.make_async_copy(h_hbm.at[pl.ds(tok, 1), :], xf.at[pl.ds(i, 1), :], sem)

    @pl.when((f == 0) & (n > 0))
    def _():
        xf[...] = jnp.zeros_like(xf)
        base = us_ref[u]

        def start(i, c):
            row_copy(tok_ref[base + i], i).start()
            return c

        lax.fori_loop(0, n, start, 0)

    @pl.when((f == 0) & (n == 0))
    def _():
        y_ref[...] = jnp.zeros_like(y_ref)

    @pl.when(n > 0)
    def _():
        wgb[...] = wg_ref[0].astype(BF16)
        wub[...] = wu_ref[0].astype(BF16)
        wdb[...] = wd_ref[0].astype(BF16)

    @pl.when((f == 0) & (n > 0))
    def _():
        def wait(i, c):
            row_copy(0, i).wait()
            return c

        lax.fori_loop(0, n, wait, 0)
        xb[...] = xf[...].astype(BF16)

    @pl.when(n > 0)
    def _():
        n_sub = (n + (MOE_SUB - 1)) // MOE_SUB
        for k in range(1, MOE_UNIT // MOE_SUB + 1):
            m_rows = k * MOE_SUB

            @pl.when(n_sub == k)
            def _():
                x = xb[0:m_rows, :]
                g = _dg(x, wgb[...])
                up = _dg(x, wub[...])
                hid = (g * _sigmoid(g) * up).astype(BF16)
                d = y_ref.shape[1]
                tn = _tile(d, MOE_DOWN_COLS, LANES)

                @pl.when(f == 0)
                def _():
                    for c0 in range(0, d, tn):
                        y_ref[0:m_rows, c0:c0 + tn] = _dg(hid, wdb[:, c0:c0 + tn])
                    if m_rows < MOE_UNIT:
                        y_ref[m_rows:MOE_UNIT, :] = jnp.zeros((MOE_UNIT - m_rows, d), F32)

                @pl.when(f > 0)
                def _():
                    for c0 in range(0, d, tn):
                        y_ref[0:m_rows, c0:c0 + tn] += _dg(hid, wdb[:, c0:c0 + tn])


def _moe_experts(h, tok_sorted, unit_e, unit_start, unit_n, w_gate, w_up, w_down):
    m, d = h.shape
    n_units = unit_e.shape[0]
    de = w_gate.shape[2]
    tf = _tile(de, 256, LANES)
    nf = de // tf

    def fidx(u, f, un):
        return jnp.where(un[u] > 0, f, nf - 1)

    return pl.pallas_call(
        _moe_kernel,
        grid_spec=pltpu.PrefetchScalarGridSpec(
            num_scalar_prefetch=4,
            grid=(n_units, nf),
            in_specs=[pl.BlockSpec(memory_space=pl.ANY),
                      pl.BlockSpec((1, d, tf), lambda u, f, ue, us, un, tok: (ue[u], 0, fidx(u, f, un))),
                      pl.BlockSpec((1, d, tf), lambda u, f, ue, us, un, tok: (ue[u], 0, fidx(u, f, un))),
                      pl.BlockSpec((1, tf, d), lambda u, f, ue, us, un, tok: (ue[u], fidx(u, f, un), 0))],
            out_specs=pl.BlockSpec((MOE_UNIT, d), lambda u, f, ue, us, un, tok: (u, 0)),
            scratch_shapes=[pltpu.VMEM((MOE_UNIT, d), F32), pltpu.VMEM((MOE_UNIT, d), BF16),
                            pltpu.VMEM((d, tf), BF16), pltpu.VMEM((d, tf), BF16),
                            pltpu.VMEM((tf, d), BF16), pltpu.SemaphoreType.DMA(())]),
        out_shape=jax.ShapeDtypeStruct((n_units * MOE_UNIT, d), F32),
        compiler_params=_params(("arbitrary", "arbitrary")),
        name="moe_experts",
    )(unit_e, unit_start, unit_n, tok_sorted, h, w_gate, w_up, w_down)


def _combine_kernel(pos_ref, ys_hbm, w_ref, x_ref, g_ref, o_ref, buf, sems, *, normalize):
    i = pl.program_id(0)
    tc = x_ref.shape[0]
    half = tc // 2
    base = i * tc * TOP_K

    def row_copy(p, s, t, part):
        return pltpu.make_async_copy(ys_hbm.at[pl.ds(p, 1), :], buf.at[s, pl.ds(t, 1), :], sems.at[part])

    def start(part):
        def body(t, c):
            for s in range(TOP_K):
                row_copy(pos_ref[base + TOP_K * t + s], s, t, part).start()
            return c
        lax.fori_loop(part * half, (part + 1) * half, body, 0)

    def wait(part):
        def body(t, c):
            for s in range(TOP_K):
                row_copy(0, s, t, part).wait()
            return c
        lax.fori_loop(part * half, (part + 1) * half, body, 0)

    def finish(part):
        rows = slice(part * half, (part + 1) * half)
        w = w_ref[rows, :]
        x = x_ref[rows, :] + (w[:, 0:1] * buf[0, rows, :] + w[:, 1:2] * buf[1, rows, :])
        if normalize:
            ms = jnp.mean(x * x, axis=-1, keepdims=True)
            x = x * lax.rsqrt(ms + RMS_EPS) * g_ref[...]
        o_ref[rows, :] = x

    start(0)
    start(1)
    wait(0)
    finish(0)
    wait(1)
    finish(1)


def _moe_combine(ys, pos, wts, x, g, normalize):
    m, d = x.shape
    tc = _tile(m, 128, 16)
    return pl.pallas_call(
        functools.partial(_combine_kernel, normalize=normalize),
        grid_spec=pltpu.PrefetchScalarGridSpec(
            num_scalar_prefetch=1,
            grid=(m // tc,),
            in_specs=[pl.BlockSpec(memory_space=pl.ANY),
                      pl.BlockSpec((tc, LANES), lambda i, p: (i, 0)),
                      pl.BlockSpec((tc, d), lambda i, p: (i, 0)),
                      pl.BlockSpec((1, d), lambda i, p: (0, 0))],
            out_specs=pl.BlockSpec((tc, d), lambda i, p: (i, 0)),
            scratch_shapes=[pltpu.VMEM((TOP_K, tc, d), F32), pltpu.SemaphoreType.DMA((2,))]),
        out_shape=jax.ShapeDtypeStruct((m, d), F32),
        compiler_params=_params(("arbitrary",)),
        name="moe_combine",
    )(pos, ys, wts, x, g.reshape(1, d))


def _moe_layer(x, norm_g, rg_w, rg_b, re_w, re_b, w_gate, w_up, w_down, final_g, normalize):
    m, d = x.shape
    n_exp = re_w.shape[1]
    h, expert, wts = _router(x, norm_g, rg_w, rg_b, re_w, re_b)

    n_rows = m * TOP_K
    flat_e = expert.reshape(-1)
    order = jnp.argsort(flat_e).astype(jnp.int32)
    e_sorted = flat_e[order]
    tok_sorted = order // TOP_K
    counts = jnp.zeros((n_exp,), jnp.int32).at[flat_e].add(1)
    start = jnp.cumsum(counts) - counts
    units_e = (counts + MOE_UNIT - 1) // MOE_UNIT
    ucum = jnp.cumsum(units_e)
    ufirst = ucum - units_e
    n_units = n_exp + -(-n_rows // MOE_UNIT)
    uid = jnp.arange(n_units, dtype=jnp.int32)
    total = ucum[-1]
    ue = jnp.minimum(jnp.searchsorted(ucum, uid, side="right"), n_exp - 1).astype(jnp.int32)
    local = uid - ufirst[ue]
    active = uid < total
    un = jnp.where(active, jnp.clip(counts[ue] - local * MOE_UNIT, 0, MOE_UNIT), 0).astype(jnp.int32)
    us = jnp.where(active, start[ue] + local * MOE_UNIT, 0).astype(jnp.int32)
    last_e = ue[jnp.maximum(total - 1, 0)]
    ue = jnp.where(active, ue, last_e).astype(jnp.int32)
    local_row = jnp.arange(n_rows, dtype=jnp.int32) - start[e_sorted]
    pos_sorted = (ufirst[e_sorted] + local_row // MOE_UNIT) * MOE_UNIT + local_row % MOE_UNIT
    pos = jnp.zeros((n_rows,), jnp.int32).at[order].set(pos_sorted.astype(jnp.int32))

    ys = _moe_experts(h, tok_sorted, ue, us, un, w_gate, w_up, w_down)
    return _moe_combine(ys, pos, wts, x, final_g, normalize)


def _mixer(x, shift0, wkv0, past, lw):
    b, t, d = x.shape
    rw_proj = lw["mu"].shape[0]
    rww = lw["w_up_a"].shape[0]
    sbw = lw["w_up_b"].shape[0]
    heads = lw["sb_bias"].shape[0]
    hd = sbw // heads
    q_off, k_off, v_off = rw_proj, rw_proj + sbw, rw_proj + 2 * sbw
    ga_off, gb_off = rw_proj + 3 * sbw, rw_proj + 3 * sbw + d
    x2 = x.reshape(b * t, d)

    h = _rmsnorm(x2, lw["norm1_g"], BF16)
    proj = _matmul(h, lw["w_in"], F32, 1024, 768)
    proj3 = proj.reshape(b, t, -1)

    r, lgw, k2, v, na, bb, bonus = _rwkv_prep(proj3, shift0, lw["mu"], lw["w0"], lw["w2"], lw["a0"],
                                              lw["a2"], lw["kk"], lw["ka"], lw["rk"])
    o_a, wkv_new = _rwkv_recurrence(r, lgw, k2, v, na, bb, bonus, wkv0, lw["ln_g"], lw["ln_b"])
    shift_new = proj3[:, -1, :rw_proj]

    k_new = proj3[:, :, k_off:k_off + sbw]
    v_new = proj3[:, :, v_off:v_off + sbw]
    if past is None:
        o_b = _sb_prompt(proj3, lw["sb_bias"], q_off, k_off, v_off, heads, hd)
    else:
        cache_k, cache_v, layer, page_table = past
        q = proj3[:, :, q_off:q_off + sbw]
        o_b = _sb_sample(q, k_new, v_new, cache_k, cache_v, layer, page_table, lw["sb_bias"]).astype(BF16)

    mixed = _merge_up(o_a.reshape(b * t, rww), o_b.reshape(b * t, sbw), proj, lw["w_up_a"], lw["w_up_b"],
                      ga_off, gb_off)
    x1 = _out_proj(mixed, lw["w_out"], x2)
    return (x1, k_new.reshape(b, t, heads, hd), v_new.reshape(b, t, heads, hd), wkv_new, shift_new)


def kernel(x_prompt, x_sample, cache_k, cache_v, state_wkv, state_shift, page_table, norm1_g, w_in, rwkv_mu, rwkv_w0, rwkv_w2, rwkv_a0, rwkv_a2, rwkv_kk, rwkv_ka, rwkv_rk, rwkv_ln_g, rwkv_ln_b, sb_bias, w_up_a, w_up_b, w_out, norm2_g, router_group_w, router_group_b, router_expert_w, router_expert_b, expert_w_gate, expert_w_up, expert_w_down, normf_g):
    depth = w_in.shape[0]
    bp, tp, d = x_prompt.shape
    bs, ts, _ = x_sample.shape
    rw_proj = rwkv_mu.shape[1]
    rww = w_up_a.shape[1]
    rw_heads = rww // RW_HEAD_DIM
    xp, xs = x_prompt, x_sample
    outs = [[] for _ in range(8)]
    for l in range(depth):
        lw = dict(norm1_g=norm1_g[l], w_in=w_in[l].astype(BF16), mu=rwkv_mu[l], w0=rwkv_w0[l],
                  w2=rwkv_w2[l], a0=rwkv_a0[l], a2=rwkv_a2[l], kk=rwkv_kk[l], ka=rwkv_ka[l],
                  rk=rwkv_rk[l].reshape(-1), ln_g=rwkv_ln_g[l], ln_b=rwkv_ln_b[l], sb_bias=sb_bias[l],
                  w_up_a=w_up_a[l].astype(BF16), w_up_b=w_up_b[l].astype(BF16),
                  w_out=w_out[l].astype(BF16))
        shift0 = jnp.zeros((bp, rw_proj), F32)
        wkv0 = jnp.zeros((bp, rw_heads, RW_HEAD_DIM, RW_HEAD_DIM), F32)
        res_p = _mixer(xp, shift0, wkv0, None, lw)
        res_s = _mixer(xs, state_shift[l], state_wkv[l], (cache_k, cache_v, l, page_table), lw)
        for i in range(4):
            outs[i].append(res_p[1 + i])
            outs[4 + i].append(res_s[1 + i])
        x1 = jnp.concatenate([res_p[0], res_s[0]], axis=0)
        last = l == depth - 1
        x2 = _moe_layer(x1, norm2_g[l], router_group_w[l], router_group_b[l], router_expert_w[l],
                        router_expert_b[l], expert_w_gate[l], expert_w_up[l], expert_w_down[l],
                        normf_g, last)
        xp = x2[:bp * tp].reshape(bp, tp, d)
        xs = x2[bp * tp:].reshape(bs, ts, d)
    return (xp, xs) + tuple(jnp.stack(o) for o in outs)
```
